```python
import jax, jax.numpy as jnp
from jax import lax
import numpy as np

D_MODEL = 2048
BATCH = 16
SEQ = 256
DEPTH = 4
DEC_BATCH = 4
DEC_SEQ = 1024
PAST_LEN = 256

GRID_W = 64
HEAD_DIM = 128
N_Q_HEADS = 8
N_KV_HEADS = 2
N_GROUP = N_Q_HEADS // N_KV_HEADS
ATTN_DIM = N_Q_HEADS * HEAD_DIM
KV_DIM = N_KV_HEADS * HEAD_DIM
Q_BLOCK = 128
ROPE_THETA = 10000.0
ROPE_AXIS_DIM = HEAD_DIM // 2
F_GROUPS = 8
F_GROUP_DIM = 128
F_DIM = F_GROUPS * F_GROUP_DIM
GLA_HEADS = 4
GLA_DK = 128
GLA_DV = 256
GLA_K_DIM = GLA_HEADS * GLA_DK
GLA_V_DIM = GLA_HEADS * GLA_DV
GLA_GATE_RANK = 16
GLA_GATE_TAU = 16.0
GLA_CHUNK = 64
D_FF = 5632
CONV_W = 3
N_MOD = 6
N_BRANCH = 3
EPS = 1e-6

IN_SPLITS = (F_DIM, ATTN_DIM, KV_DIM, KV_DIM, GLA_K_DIM, GLA_K_DIM, GLA_V_DIM, GLA_V_DIM,
             GLA_GATE_RANK, GLA_GATE_RANK, N_BRANCH * D_MODEL)
N_IN = F_DIM + ATTN_DIM + 2 * KV_DIM + 2 * GLA_K_DIM + 2 * GLA_V_DIM + 2 * GLA_GATE_RANK + N_BRANCH * D_MODEL

kernel_name = 'hybrid_fnet_gqa_gla_convffn_diffusion_step'

F32 = jnp.float32


def rms_norm(x, g):
    xf = x.astype(F32)
    y = xf * lax.rsqrt(jnp.mean(xf * xf, axis=-1, keepdims=True) + EPS)
    return (y * g.astype(F32)).astype(x.dtype)


def split_cols(z):
    idx = np.cumsum(np.array(IN_SPLITS))[:-1].tolist()
    return jnp.split(z, idx, axis=-1)


def grid_angles(T):
    rows = T // GRID_W
    row = jnp.repeat(jnp.arange(rows, dtype=F32), GRID_W)
    col = jnp.tile(jnp.arange(GRID_W, dtype=F32), rows)
    inv = ROPE_THETA ** (-jnp.arange(0, ROPE_AXIS_DIM, 2, dtype=F32) / ROPE_AXIS_DIM)
    return row[:, None] * inv, col[:, None] * inv


def _rotate(x, ang):
    cos = jnp.cos(ang)[None, :, None, :]
    sin = jnp.sin(ang)[None, :, None, :]
    x1, x2 = jnp.split(x, 2, axis=-1)
    return jnp.concatenate([x1 * cos - x2 * sin, x2 * cos + x1 * sin], axis=-1)


def axial_rope(x, ang_row, ang_col):
    xf = x.astype(F32)
    out = jnp.concatenate([_rotate(xf[..., :ROPE_AXIS_DIM], ang_row),
                           _rotate(xf[..., ROPE_AXIS_DIM:], ang_col)], axis=-1)
    return out.astype(x.dtype)


def block_attention(q, k, v):
    B, T = q.shape[0], q.shape[1]
    nb = T // Q_BLOCK
    qb = q.reshape(B, nb, Q_BLOCK, N_KV_HEADS, N_GROUP, HEAD_DIM).transpose(1, 0, 2, 3, 4, 5)
    kf = k.astype(F32)
    vf = v.astype(F32)
    scale = HEAD_DIM ** -0.5

    def one_block(qblk):
        s = jnp.einsum('bqkgd,bskd->bkgqs', qblk.astype(F32), kf) * scale
        p = jax.nn.softmax(s, axis=-1)
        return jnp.einsum('bkgqs,bskd->bqkgd', p, vf).astype(q.dtype)

    o = lax.map(one_block, qb)
    return o.transpose(1, 0, 2, 3, 4, 5).reshape(B, T, ATTN_DIM)


def gla_scan(q, k, v, log_a, s0):
    B, T = q.shape[0], q.shape[1]
    nc = T // GLA_CHUNK

    def chunks(a):
        return a.astype(F32).reshape(B, nc, GLA_CHUNK, GLA_HEADS, a.shape[-1]).transpose(1, 0, 3, 2, 4)

    mask = jnp.tril(jnp.ones((GLA_CHUNK, GLA_CHUNK), dtype=bool))

    def step(S, inp):
        qc, kc, vc, ac = inp
        b = jnp.cumsum(ac, axis=2)
        o_inter = jnp.einsum('bhtd,bhdv->bhtv', qc * jnp.exp(b), S)
        diff = b[:, :, :, None, :] - b[:, :, None, :, :]
        dec = jnp.exp(jnp.where(mask[:, :, None], diff, -jnp.inf))
        att = jnp.einsum('bhtd,bhsd,bhtsd->bhts', qc, kc, dec)
        o = o_inter + jnp.einsum('bhts,bhsv->bhtv', att, vc)
        b_last = b[:, :, -1:, :]
        S_new = jnp.exp(b_last[:, :, 0, :])[..., None] * S + jnp.einsum(
            'bhsd,bhsv->bhdv', kc * jnp.exp(b_last - b), vc)
        return S_new, o

    s_fin, o = lax.scan(step, s0.astype(F32), (chunks(q), chunks(k), chunks(v), chunks(log_a)))
    o = o.transpose(1, 0, 3, 2, 4).reshape(B, T, GLA_HEADS, GLA_DV)
    return o, s_fin


def gla_bidirectional(q, k, v, la_f, la_b, sf0, sb0):
    o_f, sf = gla_scan(q, k, v, la_f, sf0)
    o_b, sb = gla_scan(q[:, ::-1], k[:, ::-1], v[:, ::-1], la_b[:, ::-1], sb0)
    return o_f + o_b[:, ::-1], sf, sb


def token_mixers(h, lw, is_latent, ctx_k, ctx_v, sf0, sb0):
    B, T, _ = h.shape
    z = h @ lw['w_in']
    f_in, q, k, v, gq, gk, gv, gr, gaf, gab, gates = split_cols(z)

    fa = f_in.reshape(B, T, F_GROUPS, F_GROUP_DIM).astype(F32)
    fa = jnp.real(jnp.fft.fft2(fa, axes=(1, 3), norm='ortho')).reshape(B, T, F_DIM).astype(h.dtype)
    br_a = fa @ lw['w_fourier']

    q = rms_norm(q.reshape(B, T, N_Q_HEADS, HEAD_DIM), lw['q_norm'])
    k = rms_norm(k.reshape(B, T, N_KV_HEADS, HEAD_DIM), lw['k_norm'])
    v = v.reshape(B, T, N_KV_HEADS, HEAD_DIM)
    if is_latent:
        ang_r, ang_c = grid_angles(T)
        q = axial_rope(q, ang_r, ang_c)
        keys = jnp.concatenate([ctx_k.astype(k.dtype), axial_rope(k, ang_r, ang_c)], axis=1)
        vals = jnp.concatenate([ctx_v.astype(v.dtype), v], axis=1)
    else:
        keys, vals = k, v
    br_b = block_attention(q, keys, vals) @ lw['w_attn']

    gq = gq.reshape(B, T, GLA_HEADS, GLA_DK) * (GLA_DK ** -0.5)
    gk = gk.reshape(B, T, GLA_HEADS, GLA_DK)
    gv = gv.reshape(B, T, GLA_HEADS, GLA_DV)
    la_f = jax.nn.log_sigmoid((gaf @ lw['w_gate_f'] + lw['b_gate_f']).astype(F32)) / GLA_GATE_TAU
    la_b = jax.nn.log_sigmoid((gab @ lw['w_gate_b'] + lw['b_gate_b']).astype(F32)) / GLA_GATE_TAU
    la_f = la_f.reshape(B, T, GLA_HEADS, GLA_DK)
    la_b = la_b.reshape(B, T, GLA_HEADS, GLA_DK)
    o, sf, sb = gla_bidirectional(gq, gk, gv, la_f, la_b, sf0, sb0)
    o = rms_norm(o, lw['gla_norm']).reshape(B, T, GLA_V_DIM).astype(h.dtype) * jax.nn.silu(gr)
    br_c = o @ lw['w_gla']

    g_a, g_b, g_c = jnp.split(gates, N_BRANCH, axis=-1)
    merged = jax.nn.sigmoid(g_a) * br_a + jax.nn.sigmoid(g_b) * br_b + jax.nn.sigmoid(g_c) * br_c
    return merged @ lw['w_out'], k, v, sf, sb


def conv_ffn(h, lw):
    u = h @ lw['w_up']
    up = jnp.pad(u, ((0, 0), (1, 1), (0, 0)))
    cw = lw['conv_w']
    u = up[:, :-2] * cw[0] + up[:, 1:-1] * cw[1] + up[:, 2:] * cw[2] + lw['conv_b']
    g, val = jnp.split(u, 2, axis=-1)
    return (jax.nn.silu(g) * val) @ lw['w_down']


def trunk_layer(x, mod, lw, is_latent, ctx_k, ctx_v, sf0, sb0):
    shift1, scale1, gate1, shift2, scale2, gate2 = jnp.split(mod, N_MOD, axis=-1)
    h = rms_norm(x, lw['norm1']) * (1 + scale1) + shift1
    mix, k, v, sf, sb = token_mixers(h, lw, is_latent, ctx_k, ctx_v, sf0, sb0)
    x = x + gate1 * mix
    h = rms_norm(x, lw['norm2']) * (1 + scale2) + shift2
    x = x + gate2 * conv_ffn(h, lw)
    return x, k, v, sf, sb


def setup_inputs(seed: int = 0) -> dict:
    key = jax.random.key(seed)
    ks = iter(jax.random.split(key, 40))
    nrm = lambda shape, s: jax.random.normal(next(ks), shape, F32) * s
    gain = lambda shape: 1.0 + 0.01 * jax.random.normal(next(ks), shape, F32)
    return {
        'x_prompt': nrm((BATCH, SEQ, D_MODEL), 1.0),
        'x_sample': nrm((DEC_BATCH, DEC_SEQ, D_MODEL), 1.0),
        'cache_k': nrm((DEC_BATCH, DEPTH, PAST_LEN, N_KV_HEADS, HEAD_DIM), 1.0),
        'cache_v': nrm((DEC_BATCH, DEPTH, PAST_LEN, N_KV_HEADS, HEAD_DIM), 1.0),
        'state_gla_fwd': nrm((DEC_BATCH, DEPTH, GLA_HEADS, GLA_DK, GLA_DV), 2.0),
        'state_gla_bwd': nrm((DEC_BATCH, DEPTH, GLA_HEADS, GLA_DK, GLA_DV), 2.0),
        'c': nrm((DEC_BATCH, D_MODEL), 1.0),
        'c_ctx': nrm((D_MODEL,), 1.0),
        'w_ada': nrm((DEPTH, D_MODEL, N_MOD * D_MODEL), 0.5 * D_MODEL ** -0.5),
        'b_ada': nrm((DEPTH, N_MOD * D_MODEL), 0.01),
        'norm1': gain((DEPTH, D_MODEL)),
        'w_in': nrm((DEPTH, D_MODEL, N_IN), D_MODEL ** -0.5),
        'q_norm': gain((DEPTH, HEAD_DIM)),
        'k_norm': gain((DEPTH, HEAD_DIM)),
        'w_fourier': nrm((DEPTH, F_DIM, D_MODEL), F_DIM ** -0.5),
        'w_attn': nrm((DEPTH, ATTN_DIM, D_MODEL), ATTN_DIM ** -0.5),
        'w_gate_f': nrm((DEPTH, GLA_GATE_RANK, GLA_K_DIM), GLA_GATE_RANK ** -0.5),
        'b_gate_f': nrm((DEPTH, GLA_K_DIM), 0.01),
        'w_gate_b': nrm((DEPTH, GLA_GATE_RANK, GLA_K_DIM), GLA_GATE_RANK ** -0.5),
        'b_gate_b': nrm((DEPTH, GLA_K_DIM), 0.01),
        'gla_norm': gain((DEPTH, GLA_DV)),
        'w_gla': nrm((DEPTH, GLA_V_DIM, D_MODEL), GLA_V_DIM ** -0.5),
        'w_out': nrm((DEPTH, D_MODEL, D_MODEL), D_MODEL ** -0.5),
        'norm2': gain((DEPTH, D_MODEL)),
        'w_up': nrm((DEPTH, D_MODEL, 2 * D_FF), D_MODEL ** -0.5),
        'conv_w': nrm((DEPTH, CONV_W, 2 * D_FF), CONV_W ** -0.5),
        'conv_b': nrm((DEPTH, 2 * D_FF), 0.01),
        'w_down': nrm((DEPTH, D_FF, D_MODEL), D_FF ** -0.5),
        'final_norm': gain((D_MODEL,)),
    }


def reference(x_prompt, x_sample, cache_k, cache_v, state_gla_fwd, state_gla_bwd, c, c_ctx,
              w_ada, b_ada, norm1, w_in, q_norm, k_norm, w_fourier, w_attn, w_gate_f, b_gate_f,
              w_gate_b, b_gate_b, gla_norm, w_gla, w_out, norm2, w_up, conv_w, conv_b, w_down,
              final_norm):
    xp = x_prompt
    xs = x_sample
    zero_state = jnp.zeros((xp.shape[0], GLA_HEADS, GLA_DK, GLA_DV), F32)
    new_k, new_v, new_sf, new_sb = [], [], [], []
    for l in range(DEPTH):
        lw = {
            'norm1': norm1[l], 'w_in': w_in[l], 'q_norm': q_norm[l], 'k_norm': k_norm[l],
            'w_fourier': w_fourier[l], 'w_attn': w_attn[l], 'w_gate_f': w_gate_f[l],
            'b_gate_f': b_gate_f[l], 'w_gate_b': w_gate_b[l], 'b_gate_b': b_gate_b[l],
            'gla_norm': gla_norm[l], 'w_gla': w_gla[l], 'w_out': w_out[l], 'norm2': norm2[l],
            'w_up': w_up[l], 'conv_w': conv_w[l], 'conv_b': conv_b[l], 'w_down': w_down[l],
        }
        mod_ctx = (jax.nn.silu(c_ctx) @ w_ada[l] + b_ada[l])[None, None, :]
        xp, k_ctx, v_ctx, sf, sb = trunk_layer(xp, mod_ctx, lw, False, None, None, zero_state, zero_state)
        new_k.append(k_ctx)
        new_v.append(v_ctx)
        new_sf.append(sf.astype(xp.dtype))
        new_sb.append(sb.astype(xp.dtype))
        mod_lat = (jax.nn.silu(c) @ w_ada[l] + b_ada[l])[:, None, :]
        xs, _, _, _, _ = trunk_layer(xs, mod_lat, lw, True, cache_k[:, l], cache_v[:, l],
                                     state_gla_fwd[:, l], state_gla_bwd[:, l])
    y_prompt = rms_norm(xp, final_norm)
    y_sample = rms_norm(xs, final_norm)
    new_cache_k = jnp.stack(new_k, axis=1)
    new_cache_v = jnp.stack(new_v, axis=1)
    new_state_gla_fwd = jnp.stack(new_sf, axis=1)
    new_state_gla_bwd = jnp.stack(new_sb, axis=1)
    return (y_prompt, y_sample, new_cache_k, new_cache_v, new_state_gla_fwd, new_state_gla_bwd)
```

```python
import functools
import math

import numpy as np
import jax
import jax.numpy as jnp
from jax import lax
from jax.experimental import pallas as pl
from jax.experimental.pallas import tpu as pltpu

F32 = jnp.float32
BF16 = jnp.bfloat16

D_MODEL = 2048
BATCH = 16
SEQ = 256
DEPTH = 4
DEC_BATCH = 4
DEC_SEQ = 1024
PAST_LEN = 256
GRID_W = 64
HEAD_DIM = 128
N_Q_HEADS = 8
N_KV_HEADS = 2
N_GROUP = N_Q_HEADS // N_KV_HEADS
ATTN_DIM = N_Q_HEADS * HEAD_DIM
KV_DIM = N_KV_HEADS * HEAD_DIM
ROPE_THETA = 10000.0
ROPE_AXIS_DIM = HEAD_DIM // 2
F_GROUPS = 8
F_GROUP_DIM = 128
F_DIM = F_GROUPS * F_GROUP_DIM
GLA_HEADS = 4
GLA_DK = 128
GLA_DV = 256
GLA_K_DIM = GLA_HEADS * GLA_DK
GLA_V_DIM = GLA_HEADS * GLA_DV
GLA_GATE_RANK = 16
GLA_GATE_TAU = 16.0
D_FF = 5632
N_MOD = 6
N_BRANCH = 3
EPS = 1e-6

N_CTX = BATCH * SEQ
N_LAT = DEC_BATCH * DEC_SEQ
N_TOK = N_CTX + N_LAT

VMEM_CAP_BYTES = 60 * 1024 * 1024
LANES = 128

W_FIN = 0
W_Q = W_FIN + F_DIM
W_K = W_Q + ATTN_DIM
W_V = W_K + KV_DIM
W_GQ = W_V + KV_DIM
W_GK = W_GQ + GLA_K_DIM
W_GV = W_GK + GLA_K_DIM
W_GR = W_GV + GLA_V_DIM
W_RANK = W_GR + GLA_V_DIM
W_GATES = W_RANK + 2 * GLA_GATE_RANK
N_IN = W_GATES + N_BRANCH * D_MODEL
GATE_SHIFT = W_GATES - W_RANK

GLA_CHUNK = 64
GLA_LEVELS = 6
ROW_TILE = 1024
GLA_UNROLL = 4


def _params(semantics, vmem_bytes):
    return pltpu.CompilerParams(dimension_semantics=semantics,
                                vmem_limit_bytes=int(min(vmem_bytes, VMEM_CAP_BYTES)))


def _mod_row(row_start):
    return jnp.where(row_start < N_CTX, 0, 1 + (row_start - N_CTX) // DEC_SEQ)


def _sigmoid(x):
    return 1.0 / (1.0 + jnp.exp(-x))


def _log_sigmoid(x):
    return jnp.minimum(x, 0.0) - jnp.log1p(jnp.exp(-jnp.abs(x)))


def _dot(a, b):
    return jnp.dot(a, b, preferred_element_type=F32)


def _dot_nt(a, b):
    return lax.dot_general(a, b, (((1,), (1,)), ((), ())), preferred_element_type=F32)


def _dot_tn(a, b):
    return lax.dot_general(a, b, (((0,), (0,)), ((), ())), preferred_element_type=F32)


def _ada_kernel(c_ref, w_ref, b_ref, o_ref):
    c = c_ref[...]
    a = (c * _sigmoid(c)).astype(BF16)
    o_ref[...] = _dot(a, w_ref[...].astype(BF16)) + b_ref[...]


def _ada(cvec, w_ada, b_ada):
    tn = 1024
    n = N_MOD * D_MODEL
    return pl.pallas_call(
        _ada_kernel,
        grid=(DEPTH, n // tn),
        in_specs=[
            pl.BlockSpec((8, D_MODEL), lambda l, j: (0, 0)),
            pl.BlockSpec((None, D_MODEL, tn), lambda l, j: (l, 0, j)),
            pl.BlockSpec((None, 1, tn), lambda l, j: (l, 0, j)),
        ],
        out_specs=pl.BlockSpec((None, 8, tn), lambda l, j: (l, 0, j)),
        out_shape=jax.ShapeDtypeStruct((DEPTH, 8, n), F32),
        compiler_params=_params(("arbitrary", "arbitrary"), 3 * D_MODEL * tn * 4 + (8 << 20)),
        name="ada_mod",
    )(cvec, w_ada, b_ada.reshape(DEPTH, 1, n))


def _modnorm_kernel(x_ref, g_ref, shift_ref, scale_ref, o_ref):
    x = x_ref[...]
    y = x * lax.rsqrt(jnp.mean(x * x, axis=-1, keepdims=True) + EPS) * g_ref[...]
    o_ref[...] = (y * (1.0 + scale_ref[...]) + shift_ref[...]).astype(o_ref.dtype)


def _modnorm(x, gains, mod, layer, shift_col, scale_col):
    tm = 512
    return pl.pallas_call(
        _modnorm_kernel,
        grid=(N_TOK // tm,),
        in_specs=[
            pl.BlockSpec((tm, D_MODEL), lambda m: (m, 0)),
            pl.BlockSpec((None, 1, D_MODEL), lambda m: (layer, 0, 0)),
            pl.BlockSpec((None, None, 1, D_MODEL), lambda m: (layer, _mod_row(m * tm), 0, shift_col)),
            pl.BlockSpec((None, None, 1, D_MODEL), lambda m: (layer, _mod_row(m * tm), 0, scale_col)),
        ],
        out_specs=pl.BlockSpec((tm, D_MODEL), lambda m: (m, 0)),
        out_shape=jax.ShapeDtypeStruct((N_TOK, D_MODEL), BF16),
        compiler_params=_params(("arbitrary",), 6 * tm * D_MODEL * 4 + (8 << 20)),
        name="modnorm",
    )(x, gains.reshape(DEPTH, 1, D_MODEL), mod, mod)


def _matmul_kernel(a_ref, w_ref, o_ref, wbf_ref):
    @pl.when(pl.program_id(1) == 0)
    def _():
        wbf_ref[...] = w_ref[...].astype(BF16)

    o_ref[...] = _dot(a_ref[...], wbf_ref[...]).astype(o_ref.dtype)


def _matmul(a, w, layer, col0, n, tm, tn, out_dtype, name):
    m, k = a.shape
    blk0 = col0 // tn
    assert blk0 * tn == col0
    return pl.pallas_call(
        _matmul_kernel,
        grid=(n // tn, m // tm),
        in_specs=[
            pl.BlockSpec((tm, k), lambda j, i: (i, 0)),
            pl.BlockSpec((None, k, tn), lambda j, i: (layer, 0, blk0 + j)),
        ],
        out_specs=pl.BlockSpec((tm, tn), lambda j, i: (i, j)),
        out_shape=jax.ShapeDtypeStruct((m, n), out_dtype),
        scratch_shapes=[pltpu.VMEM((k, tn), BF16)],
        compiler_params=_params(("arbitrary", "arbitrary"),
                                2 * (tm * k * 2 + k * tn * 4 + tm * tn * 4) + k * tn * 2 + (8 << 20)),
        name=name,
    )(a, w)


def _gates_kernel(a_ref, w_ref, wx_ref, o_ref, wbf_ref, *, tn):
    @pl.when(pl.program_id(1) == 0)
    def _():
        k = w_ref.shape[0]
        rows_per = 256
        for r in range(0, k, rows_per):
            wide = jnp.concatenate([w_ref[r:r + rows_per, :], wx_ref[r:r + rows_per, :]], axis=1)
            shifted = pltpu.roll(wide, tn + LANES - GATE_SHIFT, 1)
            wbf_ref[r:r + rows_per, :] = shifted[:, :tn].astype(BF16)

    o_ref[...] = _dot(a_ref[...], wbf_ref[...])


def _gates_proj(a, w_in, layer):
    m, k = a.shape
    tm, tn = ROW_TILE, 512
    n = N_BRANCH * D_MODEL
    blk0 = W_RANK // tn
    xblk = tn // LANES
    assert blk0 * tn == W_RANK
    return pl.pallas_call(
        functools.partial(_gates_kernel, tn=tn),
        grid=(n // tn, m // tm),
        in_specs=[
            pl.BlockSpec((tm, k), lambda j, i: (i, 0)),
            pl.BlockSpec((None, k, tn), lambda j, i: (layer, 0, blk0 + j)),
            pl.BlockSpec((None, k, LANES), lambda j, i: (layer, 0, (blk0 + j + 1) * xblk)),
        ],
        out_specs=pl.BlockSpec((tm, tn), lambda j, i: (i, j)),
        out_shape=jax.ShapeDtypeStruct((m, n), F32),
        scratch_shapes=[pltpu.VMEM((k, tn), BF16)],
        compiler_params=_params(("arbitrary", "arbitrary"),
                                2 * (tm * k * 2 + k * (tn + LANES) * 4 + tm * tn * 4) + k * tn * 2 + (12 << 20)),
        name="gates_proj",
    )(a, w_in, w_in)


def _resid_kernel(a_ref, w_ref, x_ref, gate_ref, o_ref, wbf_ref):
    @pl.when(pl.program_id(1) == 0)
    def _():
        wbf_ref[...] = w_ref[...].astype(BF16)

    o_ref[...] = x_ref[...] + gate_ref[...] * _dot(a_ref[...], wbf_ref[...])


def _resid_proj(a, w, x, mod, layer, gate_col, tm, tn):
    m, k = a.shape
    n = D_MODEL
    gate_blk = gate_col * (D_MODEL // tn)
    return pl.pallas_call(
        _resid_kernel,
        grid=(n // tn, m // tm),
        in_specs=[
            pl.BlockSpec((tm, k), lambda j, i: (i, 0)),
            pl.BlockSpec((None, k, tn), lambda j, i: (layer, 0, j)),
            pl.BlockSpec((tm, tn), lambda j, i: (i, j)),
            pl.BlockSpec((None, None, 1, tn), lambda j, i: (layer, _mod_row(i * tm), 0, gate_blk + j)),
        ],
        out_specs=pl.BlockSpec((tm, tn), lambda j, i: (i, j)),
        out_shape=jax.ShapeDtypeStruct((m, n), F32),
        scratch_shapes=[pltpu.VMEM((k, tn), BF16)],
        compiler_params=_params(("arbitrary", "arbitrary"),
                                2 * (tm * k * 2 + k * tn * 4 + 2 * tm * tn * 4) + k * tn * 2 + (8 << 20)),
        name="resid_proj",
    )(a, w, x, mod)


def _dft_consts(t_len):
    kc = np.arange(F_GROUP_DIM)
    ang_c = 2.0 * np.pi * ((kc[:, None] * kc[None, :]) % F_GROUP_DIM) / F_GROUP_DIM
    chan = np.concatenate([np.cos(ang_c), np.sin(ang_c)], axis=1)
    kt = np.arange(t_len)
    ang_t = 2.0 * np.pi * ((kt[:, None] * kt[None, :]) % t_len) / t_len
    pos = np.concatenate([np.cos(ang_t), -np.sin(ang_t)], axis=1)
    return jnp.asarray(chan, F32), jnp.asarray(pos, F32)


def _fnet_kernel(x_ref, chan_ref, pos_ref, o_ref, u_ref, *, t_len):
    @pl.when(pl.program_id(1) == 0)
    def _():
        chan = chan_ref[...].astype(BF16)
        for g in range(F_GROUPS):
            cols = slice(g * F_GROUP_DIM, (g + 1) * F_GROUP_DIM)
            cs = _dot(x_ref[:, cols].astype(BF16), chan)
            u_ref[0:t_len, cols] = cs[:, :F_GROUP_DIM].astype(BF16)
            u_ref[t_len:2 * t_len, cols] = cs[:, F_GROUP_DIM:].astype(BF16)

    y = _dot(pos_ref[...].astype(BF16), u_ref[...]) * (1.0 / math.sqrt(t_len * F_GROUP_DIM))
    o_ref[...] = y.astype(o_ref.dtype)


def _fnet(za, row0, n_seq, t_len):
    chan, pos = _dft_consts(t_len)
    tq = 256
    nq = t_len // tq
    seq_blk0 = row0 // t_len
    return pl.pallas_call(
        functools.partial(_fnet_kernel, t_len=t_len),
        grid=(n_seq, nq),
        in_specs=[
            pl.BlockSpec((t_len, F_DIM), lambda b, i: (seq_blk0 + b, W_FIN // F_DIM)),
            pl.BlockSpec((F_GROUP_DIM, 2 * F_GROUP_DIM), lambda b, i: (0, 0)),
            pl.BlockSpec((tq, 2 * t_len), lambda b, i: (i, 0)),
        ],
        out_specs=pl.BlockSpec((tq, F_DIM), lambda b, i: (b * nq + i, 0)),
        out_shape=jax.ShapeDtypeStruct((n_seq * t_len, F_DIM), BF16),
        scratch_shapes=[pltpu.VMEM((2 * t_len, F_DIM), BF16)],
        compiler_params=_params(("arbitrary", "arbitrary"),
                                2 * (t_len * F_DIM * 4 + tq * 2 * t_len * 4 + tq * F_DIM * 2)
                                + 2 * t_len * F_DIM * 2 + tq * 2 * t_len * 2 + tq * F_DIM * 8 + (8 << 20)),
        name="fnet_%d" % t_len,
    )(za, chan, pos)


def _head_rms(x, g):
    return x * lax.rsqrt(jnp.mean(x * x, axis=-1, keepdims=True) + EPS) * g


def _rope_tables(t_len):
    rows = t_len // GRID_W
    row = np.repeat(np.arange(rows, dtype=np.float64), GRID_W)
    col = np.tile(np.arange(GRID_W, dtype=np.float64), rows)
    inv = ROPE_THETA ** (-np.arange(0, ROPE_AXIS_DIM, 2, dtype=np.float64) / ROPE_AXIS_DIM)
    ar = row[:, None] * inv
    ac = col[:, None] * inv
    cos = np.concatenate([np.cos(ar), np.cos(ar), np.cos(ac), np.cos(ac)], axis=1)
    sin = np.concatenate([-np.sin(ar), np.sin(ar), -np.sin(ac), np.sin(ac)], axis=1)
    return jnp.asarray(cos, F32), jnp.asarray(sin, F32)


def _rope(x, cos, sin):
    lane = lax.broadcasted_iota(jnp.int32, x.shape, 1)
    low = (lane % ROPE_AXIS_DIM) < (ROPE_AXIS_DIM // 2)
    partner = jnp.where(low, pltpu.roll(x, HEAD_DIM - ROPE_AXIS_DIM // 2, 1), pltpu.roll(x, ROPE_AXIS_DIM // 2, 1))
    return x * cos + partner * sin


def _softmax_pv(q, kb, vb):
    s = _dot_nt(q.astype(BF16), kb) * (HEAD_DIM ** -0.5)
    p = jnp.exp(s - jnp.max(s, axis=-1, keepdims=True))
    den = jnp.sum(p, axis=-1, keepdims=True)
    return _dot(p.astype(BF16), vb) / den


def _attn_ctx_kernel(q_ref, k_ref, v_ref, qn_ref, kn_ref, o_ref, ko_ref, vo_ref):
    k = _head_rms(k_ref[...], kn_ref[...])
    v = v_ref[...]
    ko_ref[...] = k
    vo_ref[...] = v
    kb = k.astype(BF16)
    vb = v.astype(BF16)
    for g in range(N_GROUP):
        cols = slice(g * HEAD_DIM, (g + 1) * HEAD_DIM)
        q = _head_rms(q_ref[:, cols], qn_ref[...])
        o_ref[:, cols] = _softmax_pv(q, kb, vb).astype(o_ref.dtype)


def _attn_ctx(za, q_norm, k_norm, layer):
    gw = N_GROUP * HEAD_DIM
    return pl.pallas_call(
        _attn_ctx_kernel,
        grid=(BATCH, N_KV_HEADS),
        in_specs=[
            pl.BlockSpec((SEQ, gw), lambda b, h: (b, W_Q // gw + h)),
            pl.BlockSpec((SEQ, HEAD_DIM), lambda b, h: (b, W_K // HEAD_DIM + h)),
            pl.BlockSpec((SEQ, HEAD_DIM), lambda b, h: (b, W_V // HEAD_DIM + h)),
            pl.BlockSpec((None, 1, HEAD_DIM), lambda b, h: (layer, 0, 0)),
            pl.BlockSpec((None, 1, HEAD_DIM), lambda b, h: (layer, 0, 0)),
        ],
        out_specs=[
            pl.BlockSpec((SEQ, gw), lambda b, h: (b, h)),
            pl.BlockSpec((None, SEQ, HEAD_DIM), lambda b, h: (b, 0, h)),
            pl.BlockSpec((None, SEQ, HEAD_DIM), lambda b, h: (b, 0, h)),
        ],
        out_shape=[
            jax.ShapeDtypeStruct((N_CTX, ATTN_DIM), BF16),
            jax.ShapeDtypeStruct((BATCH, SEQ, KV_DIM), F32),
            jax.ShapeDtypeStruct((BATCH, SEQ, KV_DIM), F32),
        ],
        compiler_params=_params(("arbitrary", "arbitrary"), 16 << 20),
        name="attn_ctx",
    )(za, za, za, q_norm.reshape(DEPTH, 1, HEAD_DIM), k_norm.reshape(DEPTH, 1, HEAD_DIM))


def _attn_lat_kernel(q_ref, k_ref, v_ref, ck_ref, cv_ref, qn_ref, kn_ref, cosq_ref, sinq_ref,
                     cosk_ref, sink_ref, o_ref, kb_ref, vb_ref):
    @pl.when(pl.program_id(2) == 0)
    def _():
        kb_ref[0:PAST_LEN, :] = ck_ref[...].astype(BF16)
        vb_ref[0:PAST_LEN, :] = cv_ref[...].astype(BF16)
        k = _rope(_head_rms(k_ref[...], kn_ref[...]), cosk_ref[...], sink_ref[...])
        kb_ref[PAST_LEN:, :] = k.astype(BF16)
        vb_ref[PAST_LEN:, :] = v_ref[...].astype(BF16)

    kb = kb_ref[...]
    vb = vb_ref[...]
    for g in range(N_GROUP):
        cols = slice(g * HEAD_DIM, (g + 1) * HEAD_DIM)
        q = _rope(_head_rms(q_ref[:, cols], qn_ref[...]), cosq_ref[...], sinq_ref[...])
        o_ref[:, cols] = _softmax_pv(q, kb, vb).astype(o_ref.dtype)


def _attn_lat(za, cache_k, cache_v, q_norm, k_norm, layer):
    gw = N_GROUP * HEAD_DIM
    tq = 256
    nq = DEC_SEQ // tq
    cos, sin = _rope_tables(DEC_SEQ)
    ck = cache_k.reshape(DEC_BATCH, DEPTH, PAST_LEN, KV_DIM)
    cv = cache_v.reshape(DEC_BATCH, DEPTH, PAST_LEN, KV_DIM)
    seq0 = N_CTX // DEC_SEQ
    tile0 = N_CTX // tq
    return pl.pallas_call(
        _attn_lat_kernel,
        grid=(DEC_BATCH, N_KV_HEADS, nq),
        in_specs=[
            pl.BlockSpec((tq, gw), lambda b, h, i: (tile0 + b * nq + i, W_Q // gw + h)),
            pl.BlockSpec((DEC_SEQ, HEAD_DIM), lambda b, h, i: (seq0 + b, W_K // HEAD_DIM + h)),
            pl.BlockSpec((DEC_SEQ, HEAD_DIM), lambda b, h, i: (seq0 + b, W_V // HEAD_DIM + h)),
            pl.BlockSpec((None, None, PAST_LEN, HEAD_DIM), lambda b, h, i: (b, layer, 0, h)),
            pl.BlockSpec((None, None, PAST_LEN, HEAD_DIM), lambda b, h, i: (b, layer, 0, h)),
            pl.BlockSpec((None, 1, HEAD_DIM), lambda b, h, i: (layer, 0, 0)),
            pl.BlockSpec((None, 1, HEAD_DIM), lambda b, h, i: (layer, 0, 0)),
            pl.BlockSpec((tq, HEAD_DIM), lambda b, h, i: (i, 0)),
            pl.BlockSpec((tq, HEAD_DIM), lambda b, h, i: (i, 0)),
            pl.BlockSpec((DEC_SEQ, HEAD_DIM), lambda b, h, i: (0, 0)),
            pl.BlockSpec((DEC_SEQ, HEAD_DIM), lambda b, h, i: (0, 0)),
        ],
        out_specs=pl.BlockSpec((tq, gw), lambda b, h, i: (b * nq + i, h)),
        out_shape=jax.ShapeDtypeStruct((N_LAT, ATTN_DIM), BF16),
        scratch_shapes=[pltpu.VMEM((PAST_LEN + DEC_SEQ, HEAD_DIM), BF16),
                        pltpu.VMEM((PAST_LEN + DEC_SEQ, HEAD_DIM), BF16)],
        compiler_params=_params(("arbitrary", "arbitrary", "arbitrary"), 32 << 20),
        name="attn_lat",
    )(za, za, za, ck, cv, q_norm.reshape(DEPTH, 1, HEAD_DIM), k_norm.reshape(DEPTH, 1, HEAD_DIM),
      cos, sin, cos, sin)


def _gla_consts():
    c = GLA_CHUNK
    t = np.arange(c)
    cum_f = (t[None, :] <= t[:, None]).astype(np.float32)
    cum_b = (t[None, :] >= t[:, None]).astype(np.float32)
    upper, same = [], []
    for level in range(GLA_LEVELS):
        n = c >> level
        blk = t // n
        p = blk * n + n // 2
        upper.append(np.broadcast_to((t >= p).astype(np.float32)[:, None], (c, GLA_DK)))
        same.append((blk[:, None] == blk[None, :]).astype(np.float32))
    same.append(2.0 * np.eye(c, dtype=np.float32))
    return (jnp.asarray(cum_f, BF16), jnp.asarray(cum_b, BF16),
            jnp.asarray(np.stack(upper), F32), jnp.asarray(np.stack(same), F32))


def _chunk_select(t_len):
    nc = t_len // GLA_CHUNK
    sel = (np.arange(t_len)[:, None] // GLA_CHUNK == np.arange(LANES)[None, :]).astype(np.float32)
    assert nc <= LANES
    return jnp.asarray(sel, BF16)


def _split_hi_lo(x):
    hi = x.astype(BF16)
    lo = (x - hi.astype(F32)).astype(BF16)
    return jnp.concatenate([hi, lo], axis=1)


def _pivot_rows(b, level):
    c = GLA_CHUNK
    n = c >> level
    if n >= 16:
        parts = [jnp.broadcast_to(b[s + n // 2:s + n // 2 + 1, :], (n, GLA_DK)) for s in range(0, c, n)]
        return parts[0] if len(parts) == 1 else jnp.concatenate(parts, axis=0)
    b3 = b.reshape(c // 8, 8, GLA_DK)
    if n == 8:
        return jnp.broadcast_to(b3[:, 4:5, :], b3.shape).reshape(c, GLA_DK)
    if n == 4:
        sub = lax.broadcasted_iota(jnp.int32, b3.shape, 1)
        lo = jnp.broadcast_to(b3[:, 2:3, :], b3.shape)
        hi = jnp.broadcast_to(b3[:, 6:7, :], b3.shape)
        return jnp.where(sub < 4, lo, hi).reshape(c, GLA_DK)
    row = lax.broadcasted_iota(jnp.int32, b.shape, 0)
    return jnp.where((row & 1) == 0, pltpu.roll(b, c - 1, 0), b)


def _gla_local(ci, q_ref, k_ref, v_ref, laf_ref, lab_ref, cumf_ref, cumb_ref, up_ref, same_ref,
               oi_ref, qc_ref, kv_ref):
    c = GLA_CHUNK
    rows = pl.ds(pl.multiple_of(ci * c, c), c)
    q = q_ref[rows, :] * (GLA_DK ** -0.5)
    k = k_ref[rows, :]
    v = v_ref[rows, :].astype(BF16)
    q2 = jnp.concatenate([q, q], axis=1)
    k2 = jnp.concatenate([k, k], axis=1)

    def decay(la_ref, cum_ref, exit_row):
        s = _dot(cum_ref[...], _split_hi_lo(la_ref[rows, :]))
        b = s[:, :GLA_DK] + s[:, GLA_DK:]
        w = [jnp.exp(-jnp.abs(b - _pivot_rows(b, level))) for level in range(GLA_LEVELS)]
        return w, jnp.exp(b), jnp.exp(b[exit_row:exit_row + 1, :] - b)

    wf, cum_f, rem_f = decay(laf_ref, cumf_ref, c - 1)
    wb, cum_b, rem_b = decay(lab_ref, cumb_ref, 0)
    att = _dot_nt(q.astype(BF16), k.astype(BF16)) * same_ref[GLA_LEVELS]
    for level in range(GLA_LEVELS):
        up = up_ref[level]
        dn = 1.0 - up
        wq = jnp.concatenate([wf[level] * up, wb[level] * dn], axis=1)
        wk = jnp.concatenate([wf[level] * dn, wb[level] * up], axis=1)
        att = att + _dot_nt((q2 * wq).astype(BF16), (k2 * wk).astype(BF16)) * same_ref[level]
    oi_ref[rows, :] = _dot(att.astype(BF16), v)
    qc_ref[rows, :] = (q2 * jnp.concatenate([cum_f, cum_b], axis=1)).astype(BF16)
    kd = (k2 * jnp.concatenate([rem_f, rem_b], axis=1)).astype(BF16)
    kv_ref[ci] = _dot_tn(kd, v)


def _gla_kernel(*refs, t_len, has_state):
    if has_state:
        (q_ref, k_ref, v_ref, r_ref, zr_ref, wf_ref, wb_ref, bf_ref, bb_ref, gn_ref,
         cumf_ref, cumb_ref, up_ref, same_ref, sel_ref, s0f_ref, s0b_ref,
         o_ref, sf_ref, sb_ref, laf_ref, lab_ref, oi_ref, qc_ref, kv_ref, st_ref) = refs
    else:
        (q_ref, k_ref, v_ref, r_ref, zr_ref, wf_ref, wb_ref, bf_ref, bb_ref, gn_ref,
         cumf_ref, cumb_ref, up_ref, same_ref, sel_ref,
         o_ref, sf_ref, sb_ref, laf_ref, lab_ref, oi_ref, qc_ref, kv_ref, st_ref) = refs
    nc = t_len // GLA_CHUNK
    zr = zr_ref[...].astype(BF16)
    laf = _log_sigmoid(_dot(zr, wf_ref[...]) + bf_ref[...]) * (1.0 / GLA_GATE_TAU)
    lab = _log_sigmoid(_dot(zr, wb_ref[...]) + bb_ref[...]) * (1.0 / GLA_GATE_TAU)
    laf_ref[...] = laf
    lab_ref[...] = lab

    def chunk_totals(la):
        s = _dot_tn(_split_hi_lo(la), sel_ref[...])
        return jnp.exp(s[:GLA_DK, :] + s[GLA_DK:, :])

    tot_f = chunk_totals(laf)
    tot_b = chunk_totals(lab)

    def local(ci, carry):
        _gla_local(ci, q_ref, k_ref, v_ref, laf_ref, lab_ref, cumf_ref, cumb_ref, up_ref, same_ref,
                   oi_ref, qc_ref, kv_ref)
        return carry

    lax.fori_loop(0, nc, local, 0, unroll=GLA_UNROLL)

    s = s0f_ref[...] if has_state else jnp.zeros((GLA_DK, GLA_DV), F32)
    for ci in range(nc):
        st_ref[ci, 0:GLA_DK, :] = s.astype(BF16)
        s = tot_f[:, ci:ci + 1] * s + kv_ref[ci, 0:GLA_DK, :]
    sf_ref[...] = s
    s = s0b_ref[...] if has_state else jnp.zeros((GLA_DK, GLA_DV), F32)
    for ci in reversed(range(nc)):
        st_ref[ci, GLA_DK:2 * GLA_DK, :] = s.astype(BF16)
        s = tot_b[:, ci:ci + 1] * s + kv_ref[ci, GLA_DK:2 * GLA_DK, :]
    sb_ref[...] = s

    def finish(ci, carry):
        rows = pl.ds(pl.multiple_of(ci * GLA_CHUNK, GLA_CHUNK), GLA_CHUNK)
        o = oi_ref[rows, :] + _dot(qc_ref[rows, :], st_ref[ci])
        o = o * lax.rsqrt(jnp.mean(o * o, axis=-1, keepdims=True) + EPS) * gn_ref[...]
        r = r_ref[rows, :]
        o_ref[rows, :] = (o * (r * _sigmoid(r))).astype(o_ref.dtype)
        return carry

    lax.fori_loop(0, nc, finish, 0, unroll=GLA_UNROLL)


def _gla(za, zr, wgf, wgb, b_gate_f, b_gate_b, gla_norm, layer, row0, n_seq, t_len, s0f=None, s0b=None):
    has_state = s0f is not None
    cum_f, cum_b, upper, same = _gla_consts()
    sel = _chunk_select(t_len)
    seq0 = row0 // t_len
    nc = t_len // GLA_CHUNK
    const = lambda shape: pl.BlockSpec(shape, lambda b, h: (0,) * len(shape))
    in_specs = [
        pl.BlockSpec((t_len, GLA_DK), lambda b, h: (seq0 + b, W_GQ // GLA_DK + h)),
        pl.BlockSpec((t_len, GLA_DK), lambda b, h: (seq0 + b, W_GK // GLA_DK + h)),
        pl.BlockSpec((t_len, GLA_DV), lambda b, h: (seq0 + b, W_GV // GLA_DV + h)),
        pl.BlockSpec((t_len, GLA_DV), lambda b, h: (seq0 + b, W_GR // GLA_DV + h)),
        pl.BlockSpec((t_len, LANES), lambda b, h: (seq0 + b, 0)),
        pl.BlockSpec((None, LANES, GLA_DK), lambda b, h: (layer, 0, h)),
        pl.BlockSpec((None, LANES, GLA_DK), lambda b, h: (layer, 0, h)),
        pl.BlockSpec((None, 1, GLA_DK), lambda b, h: (layer, 0, h)),
        pl.BlockSpec((None, 1, GLA_DK), lambda b, h: (layer, 0, h)),
        pl.BlockSpec((None, 1, GLA_DV), lambda b, h: (layer, 0, 0)),
        const(cum_f.shape), const(cum_b.shape), const(upper.shape), const(same.shape), const(sel.shape),
    ]
    args = [za, za, za, za, zr, wgf, wgb, b_gate_f.reshape(DEPTH, 1, GLA_K_DIM),
            b_gate_b.reshape(DEPTH, 1, GLA_K_DIM), gla_norm.reshape(DEPTH, 1, GLA_DV),
            cum_f, cum_b, upper, same, sel]
    if has_state:
        in_specs += [
            pl.BlockSpec((None, None, None, GLA_DK, GLA_DV), lambda b, h: (b, layer, h, 0, 0)),
            pl.BlockSpec((None, None, None, GLA_DK, GLA_DV), lambda b, h: (b, layer, h, 0, 0)),
        ]
        args += [s0f, s0b]
    return pl.pallas_call(
        functools.partial(_gla_kernel, t_len=t_len, has_state=has_state),
        grid=(n_seq, GLA_HEADS),
        in_specs=in_specs,
        out_specs=[
            pl.BlockSpec((t_len, GLA_DV), lambda b, h: (b, h)),
            pl.BlockSpec((None, None, GLA_DK, GLA_DV), lambda b, h: (b, h, 0, 0)),
            pl.BlockSpec((None, None, GLA_DK, GLA_DV), lambda b, h: (b, h, 0, 0)),
        ],
        out_shape=[
            jax.ShapeDtypeStruct((n_seq * t_len, GLA_V_DIM), BF16),
            jax.ShapeDtypeStruct((n_seq, GLA_HEADS, GLA_DK, GLA_DV), F32),
            jax.ShapeDtypeStruct((n_seq, GLA_HEADS, GLA_DK, GLA_DV), F32),
        ],
        scratch_shapes=[
            pltpu.VMEM((t_len, GLA_DK), F32), pltpu.VMEM((t_len, GLA_DK), F32),
            pltpu.VMEM((t_len, GLA_DV), F32), pltpu.VMEM((t_len, 2 * GLA_DK), BF16),
            pltpu.VMEM((nc, 2 * GLA_DK, GLA_DV), F32), pltpu.VMEM((nc, 2 * GLA_DK, GLA_DV), BF16),
        ],
        compiler_params=_params(("arbitrary", "arbitrary"), 40 << 20),
        name="gla_%d" % t_len,
    )(*args)


def _merge_kernel(fc_ref, fl_ref, ac_ref, al_ref, gc_ref, gl_ref, wa_ref, wb_ref, wc_ref,
                  ga_ref, gb_ref, gg_ref, o_ref, wa_s, wb_s, wc_s, *, ctx_tiles):
    i = pl.program_id(1)

    @pl.when(i == 0)
    def _():
        wa_s[...] = wa_ref[...].astype(BF16)
        wb_s[...] = wb_ref[...].astype(BF16)
        wc_s[...] = wc_ref[...].astype(BF16)

    def compute(f_ref, a_ref, g_ref):
        acc = _sigmoid(ga_ref[...]) * _dot(f_ref[...], wa_s[...])
        acc = acc + _sigmoid(gb_ref[...]) * _dot(a_ref[...], wb_s[...])
        acc = acc + _sigmoid(gg_ref[...]) * _dot(g_ref[...], wc_s[...])
        o_ref[...] = acc.astype(o_ref.dtype)

    @pl.when(i < ctx_tiles)
    def _():
        compute(fc_ref, ac_ref, gc_ref)

    @pl.when(i >= ctx_tiles)
    def _():
        compute(fl_ref, al_ref, gl_ref)


def _merge(fa, at, gl, w_fourier, w_attn, w_gla, zg, layer):
    tm, tn = 512, 512
    k = F_DIM
    nj = D_MODEL // tn
    ctx_tiles = N_CTX // tm
    ctx = pl.BlockSpec((tm, k), lambda j, i: (jnp.minimum(i, ctx_tiles - 1), 0))
    lat = pl.BlockSpec((tm, k), lambda j, i: (jnp.maximum(i - ctx_tiles, 0), 0))
    wsp = pl.BlockSpec((None, k, tn), lambda j, i: (layer, 0, j))
    gate = lambda br: pl.BlockSpec((tm, tn), lambda j, i: (i, br * nj + j))
    return pl.pallas_call(
        functools.partial(_merge_kernel, ctx_tiles=ctx_tiles),
        grid=(nj, N_TOK // tm),
        in_specs=[ctx, lat, ctx, lat, ctx, lat, wsp, wsp, wsp, gate(0), gate(1), gate(2)],
        out_specs=pl.BlockSpec((tm, tn), lambda j, i: (i, j)),
        out_shape=jax.ShapeDtypeStruct((N_TOK, D_MODEL), BF16),
        scratch_shapes=[pltpu.VMEM((k, tn), BF16)] * 3,
        compiler_params=_params(("arbitrary", "arbitrary"),
                                2 * (6 * tm * k * 2 + 3 * k * tn * 4 + 3 * tm * tn * 4 + tm * tn * 2)
                                + 3 * k * tn * 2 + 4 * tm * tn * 4 + (8 << 20)),
        name="merge",
    )(fa[0], fa[1], at[0], at[1], gl[0], gl[1], w_fourier, w_attn, w_gla, zg, zg, zg)


def _convffn_up_kernel(h_ref, wg_ref, wv_ref, cwg_ref, cwv_ref, cbg_ref, cbv_ref, o_ref, wg_s, wv_s, *, tm):
    @pl.when(pl.program_id(1) == 0)
    def _():
        wg_s[...] = wg_ref[...].astype(BF16)
        wv_s[...] = wv_ref[...].astype(BF16)

    row0 = pl.program_id(1) * tm
    seq_len = jnp.where(row0 < N_CTX, SEQ, DEC_SEQ)
    h = h_ref[...]
    pos = lax.broadcasted_iota(jnp.int32, (tm, 1), 0) & (seq_len - 1)
    has_prev = (pos != 0).astype(F32)
    has_next = (pos != seq_len - 1).astype(F32)

    def conv(u, cw_ref, cb_ref):
        prev = pltpu.roll(u, 1, 0) * has_prev
        nxt = pltpu.roll(u, tm - 1, 0) * has_next
        return prev * cw_ref[0:1, :] + u * cw_ref[1:2, :] + nxt * cw_ref[2:3, :] + cb_ref[...]

    g = conv(_dot(h, wg_s[...]), cwg_ref, cbg_ref)
    val = conv(_dot(h, wv_s[...]), cwv_ref, cbv_ref)
    o_ref[...] = (g * _sigmoid(g) * val).astype(o_ref.dtype)


def _convffn_up(h, w_up, conv_w, conv_b, layer):
    tm, tn = ROW_TILE, 512
    k = D_MODEL
    nj = D_FF // tn
    cb = conv_b.reshape(DEPTH, 1, 2 * D_FF)
    return pl.pallas_call(
        functools.partial(_convffn_up_kernel, tm=tm),
        grid=(nj, N_TOK // tm),
        in_specs=[
            pl.BlockSpec((tm, k), lambda j, i: (i, 0)),
            pl.BlockSpec((None, k, tn), lambda j, i: (layer, 0, j)),
            pl.BlockSpec((None, k, tn), lambda j, i: (layer, 0, nj + j)),
            pl.BlockSpec((None, 3, tn), lambda j, i: (layer, 0, j)),
            pl.BlockSpec((None, 3, tn), lambda j, i: (layer, 0, nj + j)),
            pl.BlockSpec((None, 1, tn), lambda j, i: (layer, 0, j)),
            pl.BlockSpec((None, 1, tn), lambda j, i: (layer, 0, nj + j)),
        ],
        out_specs=pl.BlockSpec((tm, tn), lambda j, i: (i, j)),
        out_shape=jax.ShapeDtypeStruct((N_TOK, D_FF), BF16),
        scratch_shapes=[pltpu.VMEM((k, tn), BF16)] * 2,
        compiler_params=_params(("arbitrary", "arbitrary"),
                                2 * (tm * k * 2 + 2 * k * tn * 4 + tm * tn * 2) + 2 * k * tn * 2
                                + 8 * tm * tn * 4 + (8 << 20)),
        name="convffn_up",
    )(h, w_up, w_up, conv_w, conv_w, cb, cb)


def _final_norm_kernel(x_ref, g_ref, o_ref):
    x = x_ref[...]
    o_ref[...] = x * lax.rsqrt(jnp.mean(x * x, axis=-1, keepdims=True) + EPS) * g_ref[...]


def _final_norm(x, g):
    tm = 512
    return pl.pallas_call(
        _final_norm_kernel,
        grid=(N_TOK // tm,),
        in_specs=[pl.BlockSpec((tm, D_MODEL), lambda m: (m, 0)),
                  pl.BlockSpec((1, D_MODEL), lambda m: (0, 0))],
        out_specs=pl.BlockSpec((tm, D_MODEL), lambda m: (m, 0)),
        out_shape=jax.ShapeDtypeStruct((N_TOK, D_MODEL), F32),
        compiler_params=_params(("arbitrary",), 6 * tm * D_MODEL * 4 + (8 << 20)),
        name="final_norm",
    )(x, g.reshape(1, D_MODEL))


def _pad_gate_w(w_gate, row0):
    out = jnp.zeros((DEPTH, LANES, GLA_K_DIM), BF16)
    return out.at[:, row0:row0 + GLA_GATE_RANK, :].set(w_gate.astype(BF16))


def kernel(x_prompt, x_sample, cache_k, cache_v, state_gla_fwd, state_gla_bwd, c, c_ctx, w_ada, b_ada, norm1, w_in, q_norm, k_norm, w_fourier, w_attn, w_gate_f, b_gate_f, w_gate_b, b_gate_b, gla_norm, w_gla, w_out, norm2, w_up, conv_w, conv_b, w_down, final_norm):
    x = jnp.concatenate([x_prompt.reshape(N_CTX, D_MODEL), x_sample.reshape(N_LAT, D_MODEL)], axis=0)
    cvec = jnp.concatenate([c_ctx[None, :], c, jnp.zeros((8 - 1 - DEC_BATCH, D_MODEL), F32)], axis=0)
    mod = _ada(cvec, w_ada, b_ada).reshape(DEPTH, 8, 1, N_MOD * D_MODEL)
    wgf = _pad_gate_w(w_gate_f, 0)
    wgb = _pad_gate_w(w_gate_b, GLA_GATE_RANK)

    new_k, new_v, new_sf, new_sb = [], [], [], []
    for l in range(DEPTH):
        h = _modnorm(x, norm1, mod, l, 0, 1)
        za = _matmul(h, w_in, l, 0, W_RANK, ROW_TILE, 512, F32, "in_proj")
        zr = _matmul(h, w_in, l, W_RANK, LANES, ROW_TILE, LANES, F32, "rank_proj")
        zg = _gates_proj(h, w_in, l)
        fa = (_fnet(za, 0, BATCH, SEQ), _fnet(za, N_CTX, DEC_BATCH, DEC_SEQ))
        at_ctx, k_ctx, v_ctx = _attn_ctx(za, q_norm, k_norm, l)
        at = (at_ctx, _attn_lat(za, cache_k, cache_v, q_norm, k_norm, l))
        gl_ctx, sf, sb = _gla(za, zr, wgf, wgb, b_gate_f, b_gate_b, gla_norm, l, 0, BATCH, SEQ)
        gl_lat, _, _ = _gla(za, zr, wgf, wgb, b_gate_f, b_gate_b, gla_norm, l, N_CTX, DEC_BATCH, DEC_SEQ,
                            state_gla_fwd, state_gla_bwd)
        merged = _merge(fa, at, (gl_ctx, gl_lat), w_fourier, w_attn, w_gla, zg, l)
        x = _resid_proj(merged, w_out, x, mod, l, 2, ROW_TILE, 512)
        h = _modnorm(x, norm2, mod, l, 3, 4)
        hmid = _convffn_up(h, w_up, conv_w, conv_b, l)
        x = _resid_proj(hmid, w_down, x, mod, l, 5, 512, 512)
        new_k.append(k_ctx)
        new_v.append(v_ctx)
        new_sf.append(sf)
        new_sb.append(sb)

    y = _final_norm(x, final_norm)
    y_prompt = y[:N_CTX].reshape(BATCH, SEQ, D_MODEL)
    y_sample = y[N_CTX:].reshape(DEC_BATCH, DEC_SEQ, D_MODEL)
    kv_shape = (BATCH, DEPTH, SEQ, N_KV_HEADS, HEAD_DIM)
    return (y_prompt, y_sample,
            jnp.stack(new_k, axis=1).reshape(kv_shape), jnp.stack(new_v, axis=1).reshape(kv_shape),
            jnp.stack(new_sf, axis=1), jnp.stack(new_sb, axis=1))
```

```python
import functools
import math

import numpy as np
import jax
import jax.numpy as jnp
from jax import lax
from jax.experimental import pallas as pl
from jax.experimental.pallas import tpu as pltpu

F32 = jnp.float32
BF16 = jnp.bfloat16

D_MODEL = 2048
BATCH = 16
SEQ = 256
DEPTH = 4
DEC_BATCH = 4
DEC_SEQ = 1024
PAST_LEN = 256
GRID_W = 64
HEAD_DIM = 128
N_Q_HEADS = 8
N_KV_HEADS = 2
N_GROUP = N_Q_HEADS // N_KV_HEADS
ATTN_DIM = N_Q_HEADS * HEAD_DIM
KV_DIM = N_KV_HEADS * HEAD_DIM
ROPE_THETA = 10000.0
ROPE_AXIS_DIM = HEAD_DIM // 2
F_GROUPS = 8
F_GROUP_DIM = 128
F_DIM = F_GROUPS * F_GROUP_DIM
GLA_HEADS = 4
GLA_DK = 128
GLA_DV = 256
GLA_K_DIM = GLA_HEADS * GLA_DK
GLA_V_DIM = GLA_HEADS * GLA_DV
GLA_GATE_RANK = 16
GLA_GATE_TAU = 16.0
D_FF = 5632
N_MOD = 6
N_BRANCH = 3
EPS = 1e-6

N_CTX = BATCH * SEQ
N_LAT = DEC_BATCH * DEC_SEQ
N_TOK = N_CTX + N_LAT

VMEM_CAP_BYTES = 60 * 1024 * 1024
LANES = 128

W_FIN = 0
W_Q = W_FIN + F_DIM
W_K = W_Q + ATTN_DIM
W_V = W_K + KV_DIM
W_GQ = W_V + KV_DIM
W_GK = W_GQ + GLA_K_DIM
W_GV = W_GK + GLA_K_DIM
W_GR = W_GV + GLA_V_DIM
W_RANK = W_GR + GLA_V_DIM
W_GATES = W_RANK + 2 * GLA_GATE_RANK
N_IN = W_GATES + N_BRANCH * D_MODEL

GLA_CHUNK = 64
GLA_LEVELS = 6
ROW_TILE = 1024
GLA_UNROLL = 4


def _params(semantics, vmem_bytes):
    return pltpu.CompilerParams(dimension_semantics=semantics,
                                vmem_limit_bytes=int(min(vmem_bytes, VMEM_CAP_BYTES)))


def _mod_row(row_start):
    return jnp.where(row_start < N_CTX, 0, 1 + (row_start - N_CTX) // DEC_SEQ)


def _sigmoid(x):
    return 1.0 / (1.0 + jnp.exp(-x))


def _log_sigmoid(x):
    return jnp.minimum(x, 0.0) - jnp.log1p(jnp.exp(-jnp.abs(x)))


def _dot(a, b):
    return jnp.dot(a, b, preferred_element_type=F32)


def _dot_nt(a, b):
    return lax.dot_general(a, b, (((1,), (1,)), ((), ())), preferred_element_type=F32)


def _dot_tn(a, b):
    return lax.dot_general(a, b, (((0,), (0,)), ((), ())), preferred_element_type=F32)


def _ada_kernel(c_ref, w_ref, b_ref, o_ref):
    c = c_ref[...]
    a = (c * _sigmoid(c)).astype(BF16)
    o_ref[...] = _dot(a, w_ref[...].astype(BF16)) + b_ref[...]


def _ada(cvec, w_ada, b_ada):
    tn = 1024
    n = N_MOD * D_MODEL
    return pl.pallas_call(
        _ada_kernel,
        grid=(DEPTH, n // tn),
        in_specs=[
            pl.BlockSpec((8, D_MODEL), lambda l, j: (0, 0)),
            pl.BlockSpec((None, D_MODEL, tn), lambda l, j: (l, 0, j)),
            pl.BlockSpec((None, 1, tn), lambda l, j: (l, 0, j)),
        ],
        out_specs=pl.BlockSpec((None, 8, tn), lambda l, j: (l, 0, j)),
        out_shape=jax.ShapeDtypeStruct((DEPTH, 8, n), F32),
        compiler_params=_params(("arbitrary", "arbitrary"), 3 * D_MODEL * tn * 4 + (8 << 20)),
        name="ada_mod",
    )(cvec, w_ada, b_ada.reshape(DEPTH, 1, n))


def _modnorm_kernel(x_ref, g_ref, shift_ref, scale_ref, o_ref):
    x = x_ref[...]
    y = x * lax.rsqrt(jnp.mean(x * x, axis=-1, keepdims=True) + EPS) * g_ref[...]
    o_ref[...] = (y * (1.0 + scale_ref[...]) + shift_ref[...]).astype(o_ref.dtype)


def _modnorm(x, gains, mod, layer, shift_col, scale_col):
    tm = 512
    return pl.pallas_call(
        _modnorm_kernel,
        grid=(N_TOK // tm,),
        in_specs=[
            pl.BlockSpec((tm, D_MODEL), lambda m: (m, 0)),
            pl.BlockSpec((None, 1, D_MODEL), lambda m: (layer, 0, 0)),
            pl.BlockSpec((None, None, 1, D_MODEL), lambda m: (layer, _mod_row(m * tm), 0, shift_col)),
            pl.BlockSpec((None, None, 1, D_MODEL), lambda m: (layer, _mod_row(m * tm), 0, scale_col)),
        ],
        out_specs=pl.BlockSpec((tm, D_MODEL), lambda m: (m, 0)),
        out_shape=jax.ShapeDtypeStruct((N_TOK, D_MODEL), BF16),
        compiler_params=_params(("arbitrary",), 6 * tm * D_MODEL * 4 + (8 << 20)),
        name="modnorm",
    )(x, gains.reshape(DEPTH, 1, D_MODEL), mod, mod)


IN_TILE = 1024
ZA_COLS = 6 * IN_TILE
HALF_TILE = IN_TILE // 2
GATE_SHIFT = W_GATES - W_RANK


def _in_proj_kernel(a_ref, w_ref, o_ref, wbf_ref):
    @pl.when(pl.program_id(1) == 0)
    def _():
        wbf_ref[...] = w_ref[...].astype(BF16)

    o_ref[...] = _dot_nt(a_ref[...], wbf_ref[...])


def _in_proj(a, w_t, layer):
    m, k = a.shape
    tm, tn = ROW_TILE, IN_TILE
    return pl.pallas_call(
        _in_proj_kernel,
        grid=(ZA_COLS // tn, m // tm),
        in_specs=[
            pl.BlockSpec((tm, k), lambda j, i: (i, 0)),
            pl.BlockSpec((None, tn, k), lambda j, i: (layer, j, 0)),
        ],
        out_specs=pl.BlockSpec((tm, tn), lambda j, i: (i, j)),
        out_shape=jax.ShapeDtypeStruct((m, ZA_COLS), F32),
        scratch_shapes=[pltpu.VMEM((tn, k), BF16)],
        compiler_params=_params(("arbitrary", "arbitrary"),
                                2 * (tm * k * 2 + k * tn * 4 + tm * tn * 4) + k * tn * 2 + (8 << 20)),
        name="in_proj",
    )(a, w_t)


def _gates_kernel(a_ref, wa_ref, wb_ref, wx_ref, o_ref, wbf_ref):
    @pl.when(pl.program_id(1) == 0)
    def _():
        head = HALF_TILE - GATE_SHIFT
        wbf_ref[0:head, :] = wa_ref[GATE_SHIFT:, :].astype(BF16)
        wbf_ref[head:head + HALF_TILE, :] = wb_ref[...].astype(BF16)
        wbf_ref[head + HALF_TILE:, :] = wx_ref[...].astype(BF16)

    o_ref[...] = _dot_nt(a_ref[...], wbf_ref[...])


def _gates_proj(a, w_t, layer):
    m, k = a.shape
    tm, tn = ROW_TILE, IN_TILE
    n = N_BRANCH * D_MODEL
    blk0 = W_RANK // HALF_TILE
    assert blk0 * HALF_TILE == W_RANK and HALF_TILE % GATE_SHIFT == 0
    xper = HALF_TILE // GATE_SHIFT
    return pl.pallas_call(
        _gates_kernel,
        grid=(n // tn, m // tm),
        in_specs=[
            pl.BlockSpec((tm, k), lambda j, i: (i, 0)),
            pl.BlockSpec((None, HALF_TILE, k), lambda j, i: (layer, blk0 + 2 * j, 0)),
            pl.BlockSpec((None, HALF_TILE, k), lambda j, i: (layer, blk0 + 2 * j + 1, 0)),
            pl.BlockSpec((None, GATE_SHIFT, k), lambda j, i: (layer, (blk0 + 2 * j + 2) * xper, 0)),
        ],
        out_specs=pl.BlockSpec((tm, tn), lambda j, i: (i, j)),
        out_shape=jax.ShapeDtypeStruct((m, n), F32),
        scratch_shapes=[pltpu.VMEM((tn, k), BF16)],
        compiler_params=_params(("arbitrary", "arbitrary"),
                                2 * (tm * k * 2 + k * (tn + GATE_SHIFT) * 4 + tm * tn * 4) + k * tn * 2 + (8 << 20)),
        name="gates_proj",
    )(a, w_t, w_t, w_t)


def _resid_kernel(a_ref, w_ref, x_ref, gate_ref, o_ref, wbf_ref):
    @pl.when(pl.program_id(1) == 0)
    def _():
        wbf_ref[...] = w_ref[...].astype(BF16)

    o_ref[...] = x_ref[...] + gate_ref[...] * _dot(a_ref[...], wbf_ref[...])


def _resid_proj(a, w, x, mod, layer, gate_col, tm, tn):
    m, k = a.shape
    n = D_MODEL
    gate_blk = gate_col * (D_MODEL // tn)
    return pl.pallas_call(
        _resid_kernel,
        grid=(n // tn, m // tm),
        in_specs=[
            pl.BlockSpec((tm, k), lambda j, i: (i, 0)),
            pl.BlockSpec((None, k, tn), lambda j, i: (layer, 0, j)),
            pl.BlockSpec((tm, tn), lambda j, i: (i, j)),
            pl.BlockSpec((None, None, 1, tn), lambda j, i: (layer, _mod_row(i * tm), 0, gate_blk + j)),
        ],
        out_specs=pl.BlockSpec((tm, tn), lambda j, i: (i, j)),
        out_shape=jax.ShapeDtypeStruct((m, n), F32),
        scratch_shapes=[pltpu.VMEM((k, tn), BF16)],
        compiler_params=_params(("arbitrary", "arbitrary"),
                                2 * (tm * k * 2 + k * tn * 4 + 2 * tm * tn * 4) + k * tn * 2 + (8 << 20)),
        name="resid_proj",
    )(a, w, x, mod)


def _dft_consts(t_len):
    kc = np.arange(F_GROUP_DIM)
    ang_c = 2.0 * np.pi * ((kc[:, None] * kc[None, :]) % F_GROUP_DIM) / F_GROUP_DIM
    chan = np.concatenate([np.cos(ang_c), np.sin(ang_c)], axis=1)
    kt = np.arange(t_len)
    ang_t = 2.0 * np.pi * ((kt[:, None] * kt[None, :]) % t_len) / t_len
    pos = np.concatenate([np.cos(ang_t), -np.sin(ang_t)], axis=1)
    return jnp.asarray(chan, F32), jnp.asarray(pos, F32)


def _fnet_kernel(x_ref, chan_ref, pos_ref, o_ref, u_ref, *, t_len):
    @pl.when(pl.program_id(1) == 0)
    def _():
        chan = chan_ref[...].astype(BF16)
        for g in range(F_GROUPS):
            cols = slice(g * F_GROUP_DIM, (g + 1) * F_GROUP_DIM)
            cs = _dot(x_ref[:, cols].astype(BF16), chan)
            u_ref[0:t_len, cols] = cs[:, :F_GROUP_DIM].astype(BF16)
            u_ref[t_len:2 * t_len, cols] = cs[:, F_GROUP_DIM:].astype(BF16)

    y = _dot(pos_ref[...].astype(BF16), u_ref[...]) * (1.0 / math.sqrt(t_len * F_GROUP_DIM))
    o_ref[...] = y.astype(o_ref.dtype)


def _fnet(za, row0, n_seq, t_len):
    chan, pos = _dft_consts(t_len)
    tq = 256
    nq = t_len // tq
    seq_blk0 = row0 // t_len
    return pl.pallas_call(
        functools.partial(_fnet_kernel, t_len=t_len),
        grid=(n_seq, nq),
        in_specs=[
            pl.BlockSpec((t_len, F_DIM), lambda b, i: (seq_blk0 + b, W_FIN // F_DIM)),
            pl.BlockSpec((F_GROUP_DIM, 2 * F_GROUP_DIM), lambda b, i: (0, 0)),
            pl.BlockSpec((tq, 2 * t_len), lambda b, i: (i, 0)),
        ],
        out_specs=pl.BlockSpec((tq, F_DIM), lambda b, i: (b * nq + i, 0)),
        out_shape=jax.ShapeDtypeStruct((n_seq * t_len, F_DIM), BF16),
        scratch_shapes=[pltpu.VMEM((2 * t_len, F_DIM), BF16)],
        compiler_params=_params(("arbitrary", "arbitrary"),
                                2 * (t_len * F_DIM * 4 + tq * 2 * t_len * 4 + tq * F_DIM * 2)
                                + 2 * t_len * F_DIM * 2 + tq * 2 * t_len * 2 + tq * F_DIM * 8 + (8 << 20)),
        name="fnet_%d" % t_len,
    )(za, chan, pos)


def _head_rms(x, g):
    return x * lax.rsqrt(jnp.mean(x * x, axis=-1, keepdims=True) + EPS) * g


def _rope_tables(t_len):
    rows = t_len // GRID_W
    row = np.repeat(np.arange(rows, dtype=np.float64), GRID_W)
    col = np.tile(np.arange(GRID_W, dtype=np.float64), rows)
    inv = ROPE_THETA ** (-np.arange(0, ROPE_AXIS_DIM, 2, dtype=np.float64) / ROPE_AXIS_DIM)
    ar = row[:, None] * inv
    ac = col[:, None] * inv
    cos = np.concatenate([np.cos(ar), np.cos(ar), np.cos(ac), np.cos(ac)], axis=1)
    sin = np.concatenate([-np.sin(ar), np.sin(ar), -np.sin(ac), np.sin(ac)], axis=1)
    return jnp.asarray(cos, F32), jnp.asarray(sin, F32)


def _rope(x, cos, sin):
    lane = lax.broadcasted_iota(jnp.int32, x.shape, 1)
    low = (lane % ROPE_AXIS_DIM) < (ROPE_AXIS_DIM // 2)
    partner = jnp.where(low, pltpu.roll(x, HEAD_DIM - ROPE_AXIS_DIM // 2, 1), pltpu.roll(x, ROPE_AXIS_DIM // 2, 1))
    return x * cos + partner * sin


def _softmax_pv(q, kb, vb):
    s = _dot_nt(q.astype(BF16), kb) * (HEAD_DIM ** -0.5)
    p = jnp.exp(s - jnp.max(s, axis=-1, keepdims=True))
    den = jnp.sum(p, axis=-1, keepdims=True)
    return _dot(p.astype(BF16), vb) / den


def _attn_ctx_kernel(q_ref, k_ref, v_ref, qn_ref, kn_ref, o_ref, ko_ref, vo_ref):
    k = _head_rms(k_ref[...], kn_ref[...])
    v = v_ref[...]
    ko_ref[...] = k
    vo_ref[...] = v
    kb = k.astype(BF16)
    vb = v.astype(BF16)
    for g in range(N_GROUP):
        cols = slice(g * HEAD_DIM, (g + 1) * HEAD_DIM)
        q = _head_rms(q_ref[:, cols], qn_ref[...])
        o_ref[:, cols] = _softmax_pv(q, kb, vb).astype(o_ref.dtype)


def _attn_ctx(za, q_norm, k_norm, layer):
    gw = N_GROUP * HEAD_DIM
    return pl.pallas_call(
        _attn_ctx_kernel,
        grid=(BATCH, N_KV_HEADS),
        in_specs=[
            pl.BlockSpec((SEQ, gw), lambda b, h: (b, W_Q // gw + h)),
            pl.BlockSpec((SEQ, HEAD_DIM), lambda b, h: (b, W_K // HEAD_DIM + h)),
            pl.BlockSpec((SEQ, HEAD_DIM), lambda b, h: (b, W_V // HEAD_DIM + h)),
            pl.BlockSpec((None, 1, HEAD_DIM), lambda b, h: (layer, 0, 0)),
            pl.BlockSpec((None, 1, HEAD_DIM), lambda b, h: (layer, 0, 0)),
        ],
        out_specs=[
            pl.BlockSpec((SEQ, gw), lambda b, h: (b, h)),
            pl.BlockSpec((None, SEQ, HEAD_DIM), lambda b, h: (b, 0, h)),
            pl.BlockSpec((None, SEQ, HEAD_DIM), lambda b, h: (b, 0, h)),
        ],
        out_shape=[
            jax.ShapeDtypeStruct((N_CTX, ATTN_DIM), BF16),
            jax.ShapeDtypeStruct((BATCH, SEQ, KV_DIM), F32),
            jax.ShapeDtypeStruct((BATCH, SEQ, KV_DIM), F32),
        ],
        compiler_params=_params(("arbitrary", "arbitrary"), 16 << 20),
        name="attn_ctx",
    )(za, za, za, q_norm.reshape(DEPTH, 1, HEAD_DIM), k_norm.reshape(DEPTH, 1, HEAD_DIM))


def _attn_lat_kernel(q_ref, k_ref, v_ref, ck_ref, cv_ref, qn_ref, kn_ref, cosq_ref, sinq_ref,
                     cosk_ref, sink_ref, o_ref, kb_ref, vb_ref):
    @pl.when(pl.program_id(2) == 0)
    def _():
        kb_ref[0:PAST_LEN, :] = ck_ref[...].astype(BF16)
        vb_ref[0:PAST_LEN, :] = cv_ref[...].astype(BF16)
        k = _rope(_head_rms(k_ref[...], kn_ref[...]), cosk_ref[...], sink_ref[...])
        kb_ref[PAST_LEN:, :] = k.astype(BF16)
        vb_ref[PAST_LEN:, :] = v_ref[...].astype(BF16)

    kb = kb_ref[...]
    vb = vb_ref[...]
    for g in range(N_GROUP):
        cols = slice(g * HEAD_DIM, (g + 1) * HEAD_DIM)
        q = _rope(_head_rms(q_ref[:, cols], qn_ref[...]), cosq_ref[...], sinq_ref[...])
        o_ref[:, cols] = _softmax_pv(q, kb, vb).astype(o_ref.dtype)


def _attn_lat(za, cache_k, cache_v, q_norm, k_norm, layer):
    gw = N_GROUP * HEAD_DIM
    tq = 256
    nq = DEC_SEQ // tq
    cos, sin = _rope_tables(DEC_SEQ)
    ck = cache_k.reshape(DEC_BATCH, DEPTH, PAST_LEN, KV_DIM)
    cv = cache_v.reshape(DEC_BATCH, DEPTH, PAST_LEN, KV_DIM)
    seq0 = N_CTX // DEC_SEQ
    tile0 = N_CTX // tq
    return pl.pallas_call(
        _attn_lat_kernel,
        grid=(DEC_BATCH, N_KV_HEADS, nq),
        in_specs=[
            pl.BlockSpec((tq, gw), lambda b, h, i: (tile0 + b * nq + i, W_Q // gw + h)),
            pl.BlockSpec((DEC_SEQ, HEAD_DIM), lambda b, h, i: (seq0 + b, W_K // HEAD_DIM + h)),
            pl.BlockSpec((DEC_SEQ, HEAD_DIM), lambda b, h, i: (seq0 + b, W_V // HEAD_DIM + h)),
            pl.BlockSpec((None, None, PAST_LEN, HEAD_DIM), lambda b, h, i: (b, layer, 0, h)),
            pl.BlockSpec((None, None, PAST_LEN, HEAD_DIM), lambda b, h, i: (b, layer, 0, h)),
            pl.BlockSpec((None, 1, HEAD_DIM), lambda b, h, i: (layer, 0, 0)),
            pl.BlockSpec((None, 1, HEAD_DIM), lambda b, h, i: (layer, 0, 0)),
            pl.BlockSpec((tq, HEAD_DIM), lambda b, h, i: (i, 0)),
            pl.BlockSpec((tq, HEAD_DIM), lambda b, h, i: (i, 0)),
            pl.BlockSpec((DEC_SEQ, HEAD_DIM), lambda b, h, i: (0, 0)),
            pl.BlockSpec((DEC_SEQ, HEAD_DIM), lambda b, h, i: (0, 0)),
        ],
        out_specs=pl.BlockSpec((tq, gw), lambda b, h, i: (b * nq + i, h)),
        out_shape=jax.ShapeDtypeStruct((N_LAT, ATTN_DIM), BF16),
        scratch_shapes=[pltpu.VMEM((PAST_LEN + DEC_SEQ, HEAD_DIM), BF16),
                        pltpu.VMEM((PAST_LEN + DEC_SEQ, HEAD_DIM), BF16)],
        compiler_params=_params(("arbitrary", "arbitrary", "arbitrary"), 32 << 20),
        name="attn_lat",
    )(za, za, za, ck, cv, q_norm.reshape(DEPTH, 1, HEAD_DIM), k_norm.reshape(DEPTH, 1, HEAD_DIM),
      cos, sin, cos, sin)


def _gla_consts():
    c = GLA_CHUNK
    t = np.arange(c)
    cum_f = (t[None, :] <= t[:, None]).astype(np.float32)
    cum_b = (t[None, :] >= t[:, None]).astype(np.float32)
    upper, same = [], []
    for level in range(GLA_LEVELS):
        n = c >> level
        blk = t // n
        p = blk * n + n // 2
        upper.append(np.broadcast_to((t >= p).astype(np.float32)[:, None], (c, GLA_DK)))
        same.append((blk[:, None] == blk[None, :]).astype(np.float32))
    same.append(2.0 * np.eye(c, dtype=np.float32))
    return (jnp.asarray(cum_f, BF16), jnp.asarray(cum_b, BF16),
            jnp.asarray(np.stack(upper), F32), jnp.asarray(np.stack(same), F32))


def _chunk_select(t_len):
    nc = t_len // GLA_CHUNK
    sel = (np.arange(t_len)[:, None] // GLA_CHUNK == np.arange(LANES)[None, :]).astype(np.float32)
    assert nc <= LANES
    return jnp.asarray(sel, BF16)


def _split_hi_lo(x):
    hi = x.astype(BF16)
    lo = (x - hi.astype(F32)).astype(BF16)
    return jnp.concatenate([hi, lo], axis=1)


def _pivot_rows(b, level):
    c = GLA_CHUNK
    n = c >> level
    if n >= 16:
        parts = [jnp.broadcast_to(b[s + n // 2:s + n // 2 + 1, :], (n, GLA_DK)) for s in range(0, c, n)]
        return parts[0] if len(parts) == 1 else jnp.concatenate(parts, axis=0)
    b3 = b.reshape(c // 8, 8, GLA_DK)
    if n == 8:
        return jnp.broadcast_to(b3[:, 4:5, :], b3.shape).reshape(c, GLA_DK)
    if n == 4:
        sub = lax.broadcasted_iota(jnp.int32, b3.shape, 1)
        lo = jnp.broadcast_to(b3[:, 2:3, :], b3.shape)
        hi = jnp.broadcast_to(b3[:, 6:7, :], b3.shape)
        return jnp.where(sub < 4, lo, hi).reshape(c, GLA_DK)
    row = lax.broadcasted_iota(jnp.int32, b.shape, 0)
    return jnp.where((row & 1) == 0, pltpu.roll(b, c - 1, 0), b)


def _gla_decay_stage(ci, q_ref, k_ref, bf_ref, bb_ref, up_ref, qc_ref, slot):
    qt_ref, kt_ref, kd_ref = slot
    c = GLA_CHUNK
    rows = pl.ds(pl.multiple_of(ci * c, c), c)
    q = q_ref[rows, :] * (GLA_DK ** -0.5)
    k = k_ref[rows, :]
    b_f = bf_ref[rows, :]
    b_b = bb_ref[rows, :]
    for level in range(GLA_LEVELS):
        wf = jnp.exp(-jnp.abs(b_f - _pivot_rows(b_f, level)))
        wb = jnp.exp(-jnp.abs(b_b - _pivot_rows(b_b, level)))
        up = up_ref[level]
        dn = 1.0 - up
        lv = slice(c * level, c * (level + 1))
        qt_ref[lv, :] = jnp.concatenate([q * (wf * up), q * (wb * dn)], axis=1).astype(BF16)
        kt_ref[lv, :] = jnp.concatenate([k * (wf * dn), k * (wb * up)], axis=1).astype(BF16)
    qc_ref[rows, :] = jnp.concatenate([q * jnp.exp(b_f), q * jnp.exp(b_b)], axis=1).astype(BF16)
    kd_ref[...] = jnp.concatenate([k * jnp.exp(b_f[c - 1:c, :] - b_f),
                                   k * jnp.exp(b_b[0:1, :] - b_b)], axis=1).astype(BF16)


def _gla_matmul_stage(ci, q_ref, k_ref, v_ref, same_ref, oi_ref, kv_ref, slot):
    qt_ref, kt_ref, kd_ref = slot
    c = GLA_CHUNK
    rows = pl.ds(pl.multiple_of(ci * c, c), c)
    q = (q_ref[rows, :] * (GLA_DK ** -0.5)).astype(BF16)
    k = k_ref[rows, :].astype(BF16)
    v = v_ref[rows, :].astype(BF16)
    att = _dot_nt(q, k) * same_ref[GLA_LEVELS]
    for level in range(GLA_LEVELS):
        lv = slice(c * level, c * (level + 1))
        att = att + _dot_nt(qt_ref[lv, :], kt_ref[lv, :]) * same_ref[level]
    oi_ref[rows, :] = _dot(att.astype(BF16), v)
    kv_ref[ci] = _dot_tn(kd_ref[...], v)


def _gla_kernel(*refs, t_len, has_state):
    if has_state:
        (q_ref, k_ref, v_ref, r_ref, zr_ref, wf_ref, wb_ref, bf_in_ref, bb_in_ref, gn_ref,
         cumf_ref, cumb_ref, up_ref, same_ref, sel_ref, s0f_ref, s0b_ref,
         o_ref, sf_ref, sb_ref, bf_ref, bb_ref, oi_ref, qc_ref, kv_ref, st_ref,
         qt0, kt0, kd0, qt1, kt1, kd1) = refs
    else:
        (q_ref, k_ref, v_ref, r_ref, zr_ref, wf_ref, wb_ref, bf_in_ref, bb_in_ref, gn_ref,
         cumf_ref, cumb_ref, up_ref, same_ref, sel_ref,
         o_ref, sf_ref, sb_ref, bf_ref, bb_ref, oi_ref, qc_ref, kv_ref, st_ref,
         qt0, kt0, kd0, qt1, kt1, kd1) = refs
    nc = t_len // GLA_CHUNK
    slots = ((qt0, kt0, kd0), (qt1, kt1, kd1))
    zr = zr_ref[...].astype(BF16)
    laf = _log_sigmoid(_dot(zr, wf_ref[...]) + bf_in_ref[...]) * (1.0 / GLA_GATE_TAU)
    lab = _log_sigmoid(_dot(zr, wb_ref[...]) + bb_in_ref[...]) * (1.0 / GLA_GATE_TAU)
    bf_ref[...] = laf
    bb_ref[...] = lab

    def chunk_totals(la):
        s = _dot_tn(_split_hi_lo(la), sel_ref[...])
        return jnp.exp(s[:GLA_DK, :] + s[GLA_DK:, :])

    tot_f = chunk_totals(laf)
    tot_b = chunk_totals(lab)

    def cumulate(ci, carry):
        rows = pl.ds(pl.multiple_of(ci * GLA_CHUNK, GLA_CHUNK), GLA_CHUNK)
        sf = _dot(cumf_ref[...], _split_hi_lo(bf_ref[rows, :]))
        sb = _dot(cumb_ref[...], _split_hi_lo(bb_ref[rows, :]))
        bf_ref[rows, :] = sf[:, :GLA_DK] + sf[:, GLA_DK:]
        bb_ref[rows, :] = sb[:, :GLA_DK] + sb[:, GLA_DK:]
        return carry

    lax.fori_loop(0, nc, cumulate, 0, unroll=GLA_UNROLL)

    decay = functools.partial(_gla_decay_stage, q_ref=q_ref, k_ref=k_ref, bf_ref=bf_ref, bb_ref=bb_ref,
                              up_ref=up_ref, qc_ref=qc_ref)
    matmuls = functools.partial(_gla_matmul_stage, q_ref=q_ref, k_ref=k_ref, v_ref=v_ref, same_ref=same_ref,
                                oi_ref=oi_ref, kv_ref=kv_ref)
    decay(0, slot=slots[0])
    for ci in range(nc):
        if ci + 1 < nc:
            decay(ci + 1, slot=slots[(ci + 1) % 2])
        matmuls(ci, slot=slots[ci % 2])

    s = s0f_ref[...] if has_state else jnp.zeros((GLA_DK, GLA_DV), F32)
    for ci in range(nc):
        st_ref[ci, 0:GLA_DK, :] = s.astype(BF16)
        s = tot_f[:, ci:ci + 1] * s + kv_ref[ci, 0:GLA_DK, :]
    sf_ref[...] = s
    s = s0b_ref[...] if has_state else jnp.zeros((GLA_DK, GLA_DV), F32)
    for ci in reversed(range(nc)):
        st_ref[ci, GLA_DK:2 * GLA_DK, :] = s.astype(BF16)
        s = tot_b[:, ci:ci + 1] * s + kv_ref[ci, GLA_DK:2 * GLA_DK, :]
    sb_ref[...] = s

    def finish(ci, carry):
        rows = pl.ds(pl.multiple_of(ci * GLA_CHUNK, GLA_CHUNK), GLA_CHUNK)
        o = oi_ref[rows, :] + _dot(qc_ref[rows, :], st_ref[ci])
        o = o * lax.rsqrt(jnp.mean(o * o, axis=-1, keepdims=True) + EPS) * gn_ref[...]
        r = r_ref[rows, :]
        o_ref[rows, :] = (o * (r * _sigmoid(r))).astype(o_ref.dtype)
        return carry

    lax.fori_loop(0, nc, finish, 0, unroll=GLA_UNROLL)


def _gla(za, wgf, wgb, b_gate_f, b_gate_b, gla_norm, layer, row0, n_seq, t_len, s0f=None, s0b=None):
    has_state = s0f is not None
    cum_f, cum_b, upper, same = _gla_consts()
    sel = _chunk_select(t_len)
    seq0 = row0 // t_len
    nc = t_len // GLA_CHUNK
    const = lambda shape: pl.BlockSpec(shape, lambda b, h: (0,) * len(shape))
    in_specs = [
        pl.BlockSpec((t_len, GLA_DK), lambda b, h: (seq0 + b, W_GQ // GLA_DK + h)),
        pl.BlockSpec((t_len, GLA_DK), lambda b, h: (seq0 + b, W_GK // GLA_DK + h)),
        pl.BlockSpec((t_len, GLA_DV), lambda b, h: (seq0 + b, W_GV // GLA_DV + h)),
        pl.BlockSpec((t_len, GLA_DV), lambda b, h: (seq0 + b, W_GR // GLA_DV + h)),
        pl.BlockSpec((t_len, LANES), lambda b, h: (seq0 + b, W_RANK // LANES)),
        pl.BlockSpec((None, LANES, GLA_DK), lambda b, h: (layer, 0, h)),
        pl.BlockSpec((None, LANES, GLA_DK), lambda b, h: (layer, 0, h)),
        pl.BlockSpec((None, 1, GLA_DK), lambda b, h: (layer, 0, h)),
        pl.BlockSpec((None, 1, GLA_DK), lambda b, h: (layer, 0, h)),
        pl.BlockSpec((None, 1, GLA_DV), lambda b, h: (layer, 0, 0)),
        const(cum_f.shape), const(cum_b.shape), const(upper.shape), const(same.shape), const(sel.shape),
    ]
    args = [za, za, za, za, za, wgf, wgb, b_gate_f.reshape(DEPTH, 1, GLA_K_DIM),
            b_gate_b.reshape(DEPTH, 1, GLA_K_DIM), gla_norm.reshape(DEPTH, 1, GLA_DV),
            cum_f, cum_b, upper, same, sel]
    if has_state:
        in_specs += [
            pl.BlockSpec((None, None, None, GLA_DK, GLA_DV), lambda b, h: (b, layer, h, 0, 0)),
            pl.BlockSpec((None, None, None, GLA_DK, GLA_DV), lambda b, h: (b, layer, h, 0, 0)),
        ]
        args += [s0f, s0b]
    return pl.pallas_call(
        functools.partial(_gla_kernel, t_len=t_len, has_state=has_state),
        grid=(n_seq, GLA_HEADS),
        in_specs=in_specs,
        out_specs=[
            pl.BlockSpec((t_len, GLA_DV), lambda b, h: (b, h)),
            pl.BlockSpec((None, None, GLA_DK, GLA_DV), lambda b, h: (b, h, 0, 0)),
            pl.BlockSpec((None, None, GLA_DK, GLA_DV), lambda b, h: (b, h, 0, 0)),
        ],
        out_shape=[
            jax.ShapeDtypeStruct((n_seq * t_len, GLA_V_DIM), BF16),
            jax.ShapeDtypeStruct((n_seq, GLA_HEADS, GLA_DK, GLA_DV), F32),
            jax.ShapeDtypeStruct((n_seq, GLA_HEADS, GLA_DK, GLA_DV), F32),
        ],
        scratch_shapes=[
            pltpu.VMEM((t_len, GLA_DK), F32), pltpu.VMEM((t_len, GLA_DK), F32),
            pltpu.VMEM((t_len, GLA_DV), F32), pltpu.VMEM((t_len, 2 * GLA_DK), BF16),
            pltpu.VMEM((nc, 2 * GLA_DK, GLA_DV), F32), pltpu.VMEM((nc, 2 * GLA_DK, GLA_DV), BF16),
        ] + 2 * [pltpu.VMEM((GLA_CHUNK * GLA_LEVELS, 2 * GLA_DK), BF16),
                 pltpu.VMEM((GLA_CHUNK * GLA_LEVELS, 2 * GLA_DK), BF16),
                 pltpu.VMEM((GLA_CHUNK, 2 * GLA_DK), BF16)],
        compiler_params=_params(("arbitrary", "arbitrary"), 40 << 20),
        name="gla_%d" % t_len,
    )(*args)


def _merge_kernel(fc_ref, fl_ref, ac_ref, al_ref, gc_ref, gl_ref, wa_ref, wb_ref, wc_ref,
                  ga_ref, gb_ref, gg_ref, o_ref, wa_s, wb_s, wc_s, *, ctx_tiles):
    i = pl.program_id(1)

    @pl.when(i == 0)
    def _():
        wa_s[...] = wa_ref[...].astype(BF16)
        wb_s[...] = wb_ref[...].astype(BF16)
        wc_s[...] = wc_ref[...].astype(BF16)

    def compute(f_ref, a_ref, g_ref):
        acc = _sigmoid(ga_ref[...]) * _dot(f_ref[...], wa_s[...])
        acc = acc + _sigmoid(gb_ref[...]) * _dot(a_ref[...], wb_s[...])
        acc = acc + _sigmoid(gg_ref[...]) * _dot(g_ref[...], wc_s[...])
        o_ref[...] = acc.astype(o_ref.dtype)

    @pl.when(i < ctx_tiles)
    def _():
        compute(fc_ref, ac_ref, gc_ref)

    @pl.when(i >= ctx_tiles)
    def _():
        compute(fl_ref, al_ref, gl_ref)


def _merge(fa, at, gl, w_fourier, w_attn, w_gla, zg, layer):
    tm, tn = 512, 512
    k = F_DIM
    nj = D_MODEL // tn
    ctx_tiles = N_CTX // tm
    ctx = pl.BlockSpec((tm, k), lambda j, i: (jnp.minimum(i, ctx_tiles - 1), 0))
    lat = pl.BlockSpec((tm, k), lambda j, i: (jnp.maximum(i - ctx_tiles, 0), 0))
    wsp = pl.BlockSpec((None, k, tn), lambda j, i: (layer, 0, j))
    gate = lambda br: pl.BlockSpec((tm, tn), lambda j, i: (i, br * nj + j))
    return pl.pallas_call(
        functools.partial(_merge_kernel, ctx_tiles=ctx_tiles),
        grid=(nj, N_TOK // tm),
        in_specs=[ctx, lat, ctx, lat, ctx, lat, wsp, wsp, wsp, gate(0), gate(1), gate(2)],
        out_specs=pl.BlockSpec((tm, tn), lambda j, i: (i, j)),
        out_shape=jax.ShapeDtypeStruct((N_TOK, D_MODEL), BF16),
        scratch_shapes=[pltpu.VMEM((k, tn), BF16)] * 3,
        compiler_params=_params(("arbitrary", "arbitrary"),
                                2 * (6 * tm * k * 2 + 3 * k * tn * 4 + 3 * tm * tn * 4 + tm * tn * 2)
                                + 3 * k * tn * 2 + 4 * tm * tn * 4 + (8 << 20)),
        name="merge",
    )(fa[0], fa[1], at[0], at[1], gl[0], gl[1], w_fourier, w_attn, w_gla, zg, zg, zg)


def _convffn_up_kernel(h_ref, wg_ref, wv_ref, cwg_ref, cwv_ref, cbg_ref, cbv_ref, o_ref, wg_s, wv_s, *, tm):
    @pl.when(pl.program_id(1) == 0)
    def _():
        wg_s[...] = wg_ref[...].astype(BF16)
        wv_s[...] = wv_ref[...].astype(BF16)

    row0 = pl.program_id(1) * tm
    seq_len = jnp.where(row0 < N_CTX, SEQ, DEC_SEQ)
    h = h_ref[...]
    pos = lax.broadcasted_iota(jnp.int32, (tm, 1), 0) & (seq_len - 1)
    has_prev = (pos != 0).astype(F32)
    has_next = (pos != seq_len - 1).astype(F32)

    def conv(u, cw_ref, cb_ref):
        prev = pltpu.roll(u, 1, 0) * has_prev
        nxt = pltpu.roll(u, tm - 1, 0) * has_next
        return prev * cw_ref[0:1, :] + u * cw_ref[1:2, :] + nxt * cw_ref[2:3, :] + cb_ref[...]

    g = conv(_dot(h, wg_s[...]), cwg_ref, cbg_ref)
    val = conv(_dot(h, wv_s[...]), cwv_ref, cbv_ref)
    o_ref[...] = (g * _sigmoid(g) * val).astype(o_ref.dtype)


def _convffn_up(h, w_up, conv_w, conv_b, layer):
    tm, tn = ROW_TILE, 512
    k = D_MODEL
    nj = D_FF // tn
    cb = conv_b.reshape(DEPTH, 1, 2 * D_FF)
    return pl.pallas_call(
        functools.partial(_convffn_up_kernel, tm=tm),
        grid=(nj, N_TOK // tm),
        in_specs=[
            pl.BlockSpec((tm, k), lambda j, i: (i, 0)),
            pl.BlockSpec((None, k, tn), lambda j, i: (layer, 0, j)),
            pl.BlockSpec((None, k, tn), lambda j, i: (layer, 0, nj + j)),
            pl.BlockSpec((None, 3, tn), lambda j, i: (layer, 0, j)),
            pl.BlockSpec((None, 3, tn), lambda j, i: (layer, 0, nj + j)),
            pl.BlockSpec((None, 1, tn), lambda j, i: (layer, 0, j)),
            pl.BlockSpec((None, 1, tn), lambda j, i: (layer, 0, nj + j)),
        ],
        out_specs=pl.BlockSpec((tm, tn), lambda j, i: (i, j)),
        out_shape=jax.ShapeDtypeStruct((N_TOK, D_FF), BF16),
        scratch_shapes=[pltpu.VMEM((k, tn), BF16)] * 2,
        compiler_params=_params(("arbitrary", "arbitrary"),
                                2 * (tm * k * 2 + 2 * k * tn * 4 + tm * tn * 2) + 2 * k * tn * 2
                                + 8 * tm * tn * 4 + (8 << 20)),
        name="convffn_up",
    )(h, w_up, w_up, conv_w, conv_w, cb, cb)


def _final_norm_kernel(x_ref, g_ref, o_ref):
    x = x_ref[...]
    o_ref[...] = x * lax.rsqrt(jnp.mean(x * x, axis=-1, keepdims=True) + EPS) * g_ref[...]


def _final_norm(x, g):
    tm = 512
    return pl.pallas_call(
        _final_norm_kernel,
        grid=(N_TOK // tm,),
        in_specs=[pl.BlockSpec((tm, D_MODEL), lambda m: (m, 0)),
                  pl.BlockSpec((1, D_MODEL), lambda m: (0, 0))],
        out_specs=pl.BlockSpec((tm, D_MODEL), lambda m: (m, 0)),
        out_shape=jax.ShapeDtypeStruct((N_TOK, D_MODEL), F32),
        compiler_params=_params(("arbitrary",), 6 * tm * D_MODEL * 4 + (8 << 20)),
        name="final_norm",
    )(x, g.reshape(1, D_MODEL))


def _pad_gate_w(w_gate, row0):
    out = jnp.zeros((DEPTH, LANES, GLA_K_DIM), BF16)
    return out.at[:, row0:row0 + GLA_GATE_RANK, :].set(w_gate.astype(BF16))


def kernel(x_prompt, x_sample, cache_k, cache_v, state_gla_fwd, state_gla_bwd, c, c_ctx, w_ada, b_ada, norm1, w_in, q_norm, k_norm, w_fourier, w_attn, w_gate_f, b_gate_f, w_gate_b, b_gate_b, gla_norm, w_gla, w_out, norm2, w_up, conv_w, conv_b, w_down, final_norm):
    x = jnp.concatenate([x_prompt.reshape(N_CTX, D_MODEL), x_sample.reshape(N_LAT, D_MODEL)], axis=0)
    cvec = jnp.concatenate([c_ctx[None, :], c, jnp.zeros((8 - 1 - DEC_BATCH, D_MODEL), F32)], axis=0)
    mod = _ada(cvec, w_ada, b_ada).reshape(DEPTH, 8, 1, N_MOD * D_MODEL)
    w_t = jnp.swapaxes(w_in, 1, 2)
    wgf = _pad_gate_w(w_gate_f, 0)
    wgb = _pad_gate_w(w_gate_b, GLA_GATE_RANK)

    new_k, new_v, new_sf, new_sb = [], [], [], []
    for l in range(DEPTH):
        h = _modnorm(x, norm1, mod, l, 0, 1)
        za = _in_proj(h, w_t, l)
        zg = _gates_proj(h, w_t, l)
        fa = (_fnet(za, 0, BATCH, SEQ), _fnet(za, N_CTX, DEC_BATCH, DEC_SEQ))
        at_ctx, k_ctx, v_ctx = _attn_ctx(za, q_norm, k_norm, l)
        at = (at_ctx, _attn_lat(za, cache_k, cache_v, q_norm, k_norm, l))
        gl_ctx, sf, sb = _gla(za, wgf, wgb, b_gate_f, b_gate_b, gla_norm, l, 0, BATCH, SEQ)
        gl_lat, _, _ = _gla(za, wgf, wgb, b_gate_f, b_gate_b, gla_norm, l, N_CTX, DEC_BATCH, DEC_SEQ,
                            state_gla_fwd, state_gla_bwd)
        merged = _merge(fa, at, (gl_ctx, gl_lat), w_fourier, w_attn, w_gla, zg, l)
        x = _resid_proj(merged, w_out, x, mod, l, 2, ROW_TILE, 1024)
        h = _modnorm(x, norm2, mod, l, 3, 4)
        hmid = _convffn_up(h, w_up, conv_w, conv_b, l)
        x = _resid_proj(hmid, w_down, x, mod, l, 5, 512, 512)
        new_k.append(k_ctx)
        new_v.append(v_ctx)
        new_sf.append(sf)
        new_sb.append(sb)

    y = _final_norm(x, final_norm)
    y_prompt = y[:N_CTX].reshape(BATCH, SEQ, D_MODEL)
    y_sample = y[N_CTX:].reshape(DEC_BATCH, DEC_SEQ, D_MODEL)
    kv_shape = (BATCH, DEPTH, SEQ, N_KV_HEADS, HEAD_DIM)
    return (y_prompt, y_sample,
            jnp.stack(new_k, axis=1).reshape(kv_shape), jnp.stack(new_v, axis=1).reshape(kv_shape),
            jnp.stack(new_sf, axis=1), jnp.stack(new_sb, axis=1))
```

```python
import functools
import math

import numpy as np
import jax
import jax.numpy as jnp
from jax import lax
from jax.experimental import pallas as pl
from jax.experimental.pallas import tpu as pltpu

F32 = jnp.float32
BF16 = jnp.bfloat16

D_MODEL = 2048
BATCH = 16
SEQ = 256
DEPTH = 4
DEC_BATCH = 4
DEC_SEQ = 1024
PAST_LEN = 256
GRID_W = 64
HEAD_DIM = 128
N_Q_HEADS = 8
N_KV_HEADS = 2
N_GROUP = N_Q_HEADS // N_KV_HEADS
ATTN_DIM = N_Q_HEADS * HEAD_DIM
KV_DIM = N_KV_HEADS * HEAD_DIM
ROPE_THETA = 10000.0
ROPE_AXIS_DIM = HEAD_DIM // 2
F_GROUPS = 8
F_GROUP_DIM = 128
F_DIM = F_GROUPS * F_GROUP_DIM
GLA_HEADS = 4
GLA_DK = 128
GLA_DV = 256
GLA_K_DIM = GLA_HEADS * GLA_DK
GLA_V_DIM = GLA_HEADS * GLA_DV
GLA_GATE_RANK = 16
GLA_GATE_TAU = 16.0
D_FF = 5632
N_MOD = 6
N_BRANCH = 3
EPS = 1e-6

N_CTX = BATCH * SEQ
N_LAT = DEC_BATCH * DEC_SEQ
N_TOK = N_CTX + N_LAT

VMEM_CAP_BYTES = 60 * 1024 * 1024
LANES = 128

W_FIN = 0
W_Q = W_FIN + F_DIM
W_K = W_Q + ATTN_DIM
W_V = W_K + KV_DIM
W_GQ = W_V + KV_DIM
W_GK = W_GQ + GLA_K_DIM
W_GV = W_GK + GLA_K_DIM
W_GR = W_GV + GLA_V_DIM
W_RANK = W_GR + GLA_V_DIM
W_GATES = W_RANK + 2 * GLA_GATE_RANK
N_IN = W_GATES + N_BRANCH * D_MODEL

GLA_CHUNK = 64
GLA_LEVELS = 6
ROW_TILE = 1024
GLA_UNROLL = 4


def _params(semantics, vmem_bytes):
    return pltpu.CompilerParams(dimension_semantics=semantics,
                                vmem_limit_bytes=int(min(vmem_bytes, VMEM_CAP_BYTES)))


def _mod_row(row_start):
    return jnp.where(row_start < N_CTX, 0, 1 + (row_start - N_CTX) // DEC_SEQ)


def _sigmoid(x):
    return 1.0 / (1.0 + jnp.exp(-x))


def _log_sigmoid(x):
    return jnp.minimum(x, 0.0) - jnp.log1p(jnp.exp(-jnp.abs(x)))


def _dot(a, b):
    return jnp.dot(a, b, preferred_element_type=F32)


def _dot_nt(a, b):
    return lax.dot_general(a, b, (((1,), (1,)), ((), ())), preferred_element_type=F32)


def _dot_tn(a, b):
    return lax.dot_general(a, b, (((0,), (0,)), ((), ())), preferred_element_type=F32)


def _ada_kernel(c_ref, w_ref, b_ref, o_ref):
    c = c_ref[...]
    a = (c * _sigmoid(c)).astype(BF16)
    o_ref[...] = _dot(a, w_ref[...].astype(BF16)) + b_ref[...]


def _ada(cvec, w_ada, b_ada):
    tn = 1024
    n = N_MOD * D_MODEL
    return pl.pallas_call(
        _ada_kernel,
        grid=(DEPTH, n // tn),
        in_specs=[
            pl.BlockSpec((8, D_MODEL), lambda l, j: (0, 0)),
            pl.BlockSpec((None, D_MODEL, tn), lambda l, j: (l, 0, j)),
            pl.BlockSpec((None, 1, tn), lambda l, j: (l, 0, j)),
        ],
        out_specs=pl.BlockSpec((None, 8, tn), lambda l, j: (l, 0, j)),
        out_shape=jax.ShapeDtypeStruct((DEPTH, 8, n), F32),
        compiler_params=_params(("arbitrary", "arbitrary"), 3 * D_MODEL * tn * 4 + (8 << 20)),
        name="ada_mod",
    )(cvec, w_ada, b_ada.reshape(DEPTH, 1, n))


def _modnorm_kernel(x_ref, g_ref, shift_ref, scale_ref, o_ref):
    x = x_ref[...]
    y = x * lax.rsqrt(jnp.mean(x * x, axis=-1, keepdims=True) + EPS) * g_ref[...]
    o_ref[...] = (y * (1.0 + scale_ref[...]) + shift_ref[...]).astype(o_ref.dtype)


def _modnorm(x, gains, mod, layer, shift_col, scale_col):
    tm = 512
    return pl.pallas_call(
        _modnorm_kernel,
        grid=(N_TOK // tm,),
        in_specs=[
            pl.BlockSpec((tm, D_MODEL), lambda m: (m, 0)),
            pl.BlockSpec((None, 1, D_MODEL), lambda m: (layer, 0, 0)),
            pl.BlockSpec((None, None, 1, D_MODEL), lambda m: (layer, _mod_row(m * tm), 0, shift_col)),
            pl.BlockSpec((None, None, 1, D_MODEL), lambda m: (layer, _mod_row(m * tm), 0, scale_col)),
        ],
        out_specs=pl.BlockSpec((tm, D_MODEL), lambda m: (m, 0)),
        out_shape=jax.ShapeDtypeStruct((N_TOK, D_MODEL), BF16),
        compiler_params=_params(("arbitrary",), 6 * tm * D_MODEL * 4 + (8 << 20)),
        name="modnorm",
    )(x, gains.reshape(DEPTH, 1, D_MODEL), mod, mod)


IN_TILE = 1024
ZA_COLS = 6 * IN_TILE
HALF_TILE = IN_TILE // 2
GATE_SHIFT = W_GATES - W_RANK


def _in_proj_kernel(a_ref, w_ref, o_ref, wbf_ref):
    @pl.when(pl.program_id(1) == 0)
    def _():
        wbf_ref[...] = w_ref[...].astype(BF16)

    o_ref[...] = _dot_nt(a_ref[...], wbf_ref[...])


def _in_proj(a, w_t, layer):
    m, k = a.shape
    tm, tn = ROW_TILE, IN_TILE
    return pl.pallas_call(
        _in_proj_kernel,
        grid=(ZA_COLS // tn, m // tm),
        in_specs=[
            pl.BlockSpec((tm, k), lambda j, i: (i, 0)),
            pl.BlockSpec((None, tn, k), lambda j, i: (layer, j, 0)),
        ],
        out_specs=pl.BlockSpec((tm, tn), lambda j, i: (i, j)),
        out_shape=jax.ShapeDtypeStruct((m, ZA_COLS), F32),
        scratch_shapes=[pltpu.VMEM((tn, k), BF16)],
        compiler_params=_params(("arbitrary", "arbitrary"),
                                2 * (tm * k * 2 + k * tn * 4 + tm * tn * 4) + k * tn * 2 + (8 << 20)),
        name="in_proj",
    )(a, w_t)


def _gates_kernel(a_ref, wa_ref, wb_ref, wx_ref, o_ref, wbf_ref):
    @pl.when(pl.program_id(1) == 0)
    def _():
        head = HALF_TILE - GATE_SHIFT
        wbf_ref[0:head, :] = wa_ref[GATE_SHIFT:, :].astype(BF16)
        wbf_ref[head:head + HALF_TILE, :] = wb_ref[...].astype(BF16)
        wbf_ref[head + HALF_TILE:, :] = wx_ref[...].astype(BF16)

    o_ref[...] = _dot_nt(a_ref[...], wbf_ref[...]).astype(o_ref.dtype)


def _gates_proj(a, w_t, layer):
    m, k = a.shape
    tm, tn = ROW_TILE, IN_TILE
    n = N_BRANCH * D_MODEL
    blk0 = W_RANK // HALF_TILE
    assert blk0 * HALF_TILE == W_RANK and HALF_TILE % GATE_SHIFT == 0
    xper = HALF_TILE // GATE_SHIFT
    return pl.pallas_call(
        _gates_kernel,
        grid=(n // tn, m // tm),
        in_specs=[
            pl.BlockSpec((tm, k), lambda j, i: (i, 0)),
            pl.BlockSpec((None, HALF_TILE, k), lambda j, i: (layer, blk0 + 2 * j, 0)),
            pl.BlockSpec((None, HALF_TILE, k), lambda j, i: (layer, blk0 + 2 * j + 1, 0)),
            pl.BlockSpec((None, GATE_SHIFT, k), lambda j, i: (layer, (blk0 + 2 * j + 2) * xper, 0)),
        ],
        out_specs=pl.BlockSpec((tm, tn), lambda j, i: (i, j)),
        out_shape=jax.ShapeDtypeStruct((m, n), BF16),
        scratch_shapes=[pltpu.VMEM((tn, k), BF16)],
        compiler_params=_params(("arbitrary", "arbitrary"),
                                2 * (tm * k * 2 + k * (tn + GATE_SHIFT) * 4 + tm * tn * 4) + k * tn * 2 + (8 << 20)),
        name="gates_proj",
    )(a, w_t, w_t, w_t)


def _resid_kernel(a_ref, w_ref, x_ref, gate_ref, o_ref, wbf_ref):
    @pl.when(pl.program_id(1) == 0)
    def _():
        wbf_ref[...] = w_ref[...].astype(BF16)

    o_ref[...] = x_ref[...] + gate_ref[...] * _dot(a_ref[...], wbf_ref[...])


def _resid_proj(a, w, x, mod, layer, gate_col, tm, tn):
    m, k = a.shape
    n = D_MODEL
    gate_blk = gate_col * (D_MODEL // tn)
    return pl.pallas_call(
        _resid_kernel,
        grid=(n // tn, m // tm),
        in_specs=[
            pl.BlockSpec((tm, k), lambda j, i: (i, 0)),
            pl.BlockSpec((None, k, tn), lambda j, i: (layer, 0, j)),
            pl.BlockSpec((tm, tn), lambda j, i: (i, j)),
            pl.BlockSpec((None, None, 1, tn), lambda j, i: (layer, _mod_row(i * tm), 0, gate_blk + j)),
        ],
        out_specs=pl.BlockSpec((tm, tn), lambda j, i: (i, j)),
        out_shape=jax.ShapeDtypeStruct((m, n), F32),
        scratch_shapes=[pltpu.VMEM((k, tn), BF16)],
        compiler_params=_params(("arbitrary", "arbitrary"),
                                2 * (tm * k * 2 + k * tn * 4 + 2 * tm * tn * 4) + k * tn * 2 + (8 << 20)),
        name="resid_proj",
    )(a, w, x, mod)


def _dft_consts(t_len):
    kc = np.arange(F_GROUP_DIM)
    ang_c = 2.0 * np.pi * ((kc[:, None] * kc[None, :]) % F_GROUP_DIM) / F_GROUP_DIM
    chan = np.concatenate([np.cos(ang_c), np.sin(ang_c)], axis=1)
    kt = np.arange(t_len)
    ang_t = 2.0 * np.pi * ((kt[:, None] * kt[None, :]) % t_len) / t_len
    pos = np.concatenate([np.cos(ang_t), -np.sin(ang_t)], axis=1)
    return jnp.asarray(chan, F32), jnp.asarray(pos, F32)


def _fnet_kernel(x_ref, chan_ref, pos_ref, o_ref, u_ref, *, t_len):
    @pl.when(pl.program_id(1) == 0)
    def _():
        chan = chan_ref[...].astype(BF16)
        for g in range(F_GROUPS):
            cols = slice(g * F_GROUP_DIM, (g + 1) * F_GROUP_DIM)
            cs = _dot(x_ref[:, cols].astype(BF16), chan)
            u_ref[0:t_len, cols] = cs[:, :F_GROUP_DIM].astype(BF16)
            u_ref[t_len:2 * t_len, cols] = cs[:, F_GROUP_DIM:].astype(BF16)

    y = _dot(pos_ref[...].astype(BF16), u_ref[...]) * (1.0 / math.sqrt(t_len * F_GROUP_DIM))
    o_ref[...] = y.astype(o_ref.dtype)


def _fnet(za, row0, n_seq, t_len):
    chan, pos = _dft_consts(t_len)
    tq = 256
    nq = t_len // tq
    seq_blk0 = row0 // t_len
    return pl.pallas_call(
        functools.partial(_fnet_kernel, t_len=t_len),
        grid=(n_seq, nq),
        in_specs=[
            pl.BlockSpec((t_len, F_DIM), lambda b, i: (seq_blk0 + b, W_FIN // F_DIM)),
            pl.BlockSpec((F_GROUP_DIM, 2 * F_GROUP_DIM), lambda b, i: (0, 0)),
            pl.BlockSpec((tq, 2 * t_len), lambda b, i: (i, 0)),
        ],
        out_specs=pl.BlockSpec((tq, F_DIM), lambda b, i: (b * nq + i, 0)),
        out_shape=jax.ShapeDtypeStruct((n_seq * t_len, F_DIM), BF16),
        scratch_shapes=[pltpu.VMEM((2 * t_len, F_DIM), BF16)],
        compiler_params=_params(("arbitrary", "arbitrary"),
                                2 * (t_len * F_DIM * 4 + tq * 2 * t_len * 4 + tq * F_DIM * 2)
                                + 2 * t_len * F_DIM * 2 + tq * 2 * t_len * 2 + tq * F_DIM * 8 + (8 << 20)),
        name="fnet_%d" % t_len,
    )(za, chan, pos)


def _head_rms(x, g):
    return x * lax.rsqrt(jnp.mean(x * x, axis=-1, keepdims=True) + EPS) * g


def _rope_tables(t_len):
    rows = t_len // GRID_W
    row = np.repeat(np.arange(rows, dtype=np.float64), GRID_W)
    col = np.tile(np.arange(GRID_W, dtype=np.float64), rows)
    inv = ROPE_THETA ** (-np.arange(0, ROPE_AXIS_DIM, 2, dtype=np.float64) / ROPE_AXIS_DIM)
    ar = row[:, None] * inv
    ac = col[:, None] * inv
    cos = np.concatenate([np.cos(ar), np.cos(ar), np.cos(ac), np.cos(ac)], axis=1)
    sin = np.concatenate([-np.sin(ar), np.sin(ar), -np.sin(ac), np.sin(ac)], axis=1)
    return jnp.asarray(cos, F32), jnp.asarray(sin, F32)


def _rope(x, cos, sin):
    lane = lax.broadcasted_iota(jnp.int32, x.shape, 1)
    low = (lane % ROPE_AXIS_DIM) < (ROPE_AXIS_DIM // 2)
    partner = jnp.where(low, pltpu.roll(x, HEAD_DIM - ROPE_AXIS_DIM // 2, 1), pltpu.roll(x, ROPE_AXIS_DIM // 2, 1))
    return x * cos + partner * sin


def _softmax_pv(q, kb, vb):
    s = _dot_nt(q.astype(BF16), kb) * (HEAD_DIM ** -0.5)
    p = jnp.exp(s - jnp.max(s, axis=-1, keepdims=True))
    den = jnp.sum(p, axis=-1, keepdims=True)
    return _dot(p.astype(BF16), vb) / den


def _attn_ctx_kernel(q_ref, k_ref, v_ref, qn_ref, kn_ref, o_ref, ko_ref, vo_ref):
    k = _head_rms(k_ref[...], kn_ref[...])
    v = v_ref[...]
    ko_ref[...] = k
    vo_ref[...] = v
    kb = k.astype(BF16)
    vb = v.astype(BF16)
    for g in range(N_GROUP):
        cols = slice(g * HEAD_DIM, (g + 1) * HEAD_DIM)
        q = _head_rms(q_ref[:, cols], qn_ref[...])
        o_ref[:, cols] = _softmax_pv(q, kb, vb).astype(o_ref.dtype)


def _attn_ctx(za, q_norm, k_norm, layer):
    gw = N_GROUP * HEAD_DIM
    return pl.pallas_call(
        _attn_ctx_kernel,
        grid=(BATCH, N_KV_HEADS),
        in_specs=[
            pl.BlockSpec((SEQ, gw), lambda b, h: (b, W_Q // gw + h)),
            pl.BlockSpec((SEQ, HEAD_DIM), lambda b, h: (b, W_K // HEAD_DIM + h)),
            pl.BlockSpec((SEQ, HEAD_DIM), lambda b, h: (b, W_V // HEAD_DIM + h)),
            pl.BlockSpec((None, 1, HEAD_DIM), lambda b, h: (layer, 0, 0)),
            pl.BlockSpec((None, 1, HEAD_DIM), lambda b, h: (layer, 0, 0)),
        ],
        out_specs=[
            pl.BlockSpec((SEQ, gw), lambda b, h: (b, h)),
            pl.BlockSpec((None, SEQ, HEAD_DIM), lambda b, h: (b, 0, h)),
            pl.BlockSpec((None, SEQ, HEAD_DIM), lambda b, h: (b, 0, h)),
        ],
        out_shape=[
            jax.ShapeDtypeStruct((N_CTX, ATTN_DIM), BF16),
            jax.ShapeDtypeStruct((BATCH, SEQ, KV_DIM), F32),
            jax.ShapeDtypeStruct((BATCH, SEQ, KV_DIM), F32),
        ],
        compiler_params=_params(("arbitrary", "arbitrary"), 16 << 20),
        name="attn_ctx",
    )(za, za, za, q_norm.reshape(DEPTH, 1, HEAD_DIM), k_norm.reshape(DEPTH, 1, HEAD_DIM))


def _attn_lat_kernel(q_ref, k_ref, v_ref, ck_ref, cv_ref, qn_ref, kn_ref, cosq_ref, sinq_ref,
                     cosk_ref, sink_ref, o_ref, kb_ref, vb_ref):
    @pl.when(pl.program_id(2) == 0)
    def _():
        kb_ref[0:PAST_LEN, :] = ck_ref[...].astype(BF16)
        vb_ref[0:PAST_LEN, :] = cv_ref[...].astype(BF16)
        k = _rope(_head_rms(k_ref[...], kn_ref[...]), cosk_ref[...], sink_ref[...])
        kb_ref[PAST_LEN:, :] = k.astype(BF16)
        vb_ref[PAST_LEN:, :] = v_ref[...].astype(BF16)

    kb = kb_ref[...]
    vb = vb_ref[...]
    for g in range(N_GROUP):
        cols = slice(g * HEAD_DIM, (g + 1) * HEAD_DIM)
        q = _rope(_head_rms(q_ref[:, cols], qn_ref[...]), cosq_ref[...], sinq_ref[...])
        o_ref[:, cols] = _softmax_pv(q, kb, vb).astype(o_ref.dtype)


def _attn_lat(za, cache_k, cache_v, q_norm, k_norm, layer):
    gw = N_GROUP * HEAD_DIM
    tq = 256
    nq = DEC_SEQ // tq
    cos, sin = _rope_tables(DEC_SEQ)
    ck = cache_k.reshape(DEC_BATCH, DEPTH, PAST_LEN, KV_DIM)
    cv = cache_v.reshape(DEC_BATCH, DEPTH, PAST_LEN, KV_DIM)
    seq0 = N_CTX // DEC_SEQ
    tile0 = N_CTX // tq
    return pl.pallas_call(
        _attn_lat_kernel,
        grid=(DEC_BATCH, N_KV_HEADS, nq),
        in_specs=[
            pl.BlockSpec((tq, gw), lambda b, h, i: (tile0 + b * nq + i, W_Q // gw + h)),
            pl.BlockSpec((DEC_SEQ, HEAD_DIM), lambda b, h, i: (seq0 + b, W_K // HEAD_DIM + h)),
            pl.BlockSpec((DEC_SEQ, HEAD_DIM), lambda b, h, i: (seq0 + b, W_V // HEAD_DIM + h)),
            pl.BlockSpec((None, None, PAST_LEN, HEAD_DIM), lambda b, h, i: (b, layer, 0, h)),
            pl.BlockSpec((None, None, PAST_LEN, HEAD_DIM), lambda b, h, i: (b, layer, 0, h)),
            pl.BlockSpec((None, 1, HEAD_DIM), lambda b, h, i: (layer, 0, 0)),
            pl.BlockSpec((None, 1, HEAD_DIM), lambda b, h, i: (layer, 0, 0)),
            pl.BlockSpec((tq, HEAD_DIM), lambda b, h, i: (i, 0)),
            pl.BlockSpec((tq, HEAD_DIM), lambda b, h, i: (i, 0)),
            pl.BlockSpec((DEC_SEQ, HEAD_DIM), lambda b, h, i: (0, 0)),
            pl.BlockSpec((DEC_SEQ, HEAD_DIM), lambda b, h, i: (0, 0)),
        ],
        out_specs=pl.BlockSpec((tq, gw), lambda b, h, i: (b * nq + i, h)),
        out_shape=jax.ShapeDtypeStruct((N_LAT, ATTN_DIM), BF16),
        scratch_shapes=[pltpu.VMEM((PAST_LEN + DEC_SEQ, HEAD_DIM), BF16),
                        pltpu.VMEM((PAST_LEN + DEC_SEQ, HEAD_DIM), BF16)],
        compiler_params=_params(("arbitrary", "arbitrary", "arbitrary"), 32 << 20),
        name="attn_lat",
    )(za, za, za, ck, cv, q_norm.reshape(DEPTH, 1, HEAD_DIM), k_norm.reshape(DEPTH, 1, HEAD_DIM),
      cos, sin, cos, sin)


def _gla_consts():
    c = GLA_CHUNK
    t = np.arange(c)
    cum_f = (t[None, :] <= t[:, None]).astype(np.float32)
    cum_b = (t[None, :] >= t[:, None]).astype(np.float32)
    upper, same = [], []
    for level in range(GLA_LEVELS):
        n = c >> level
        blk = t // n
        p = blk * n + n // 2
        upper.append(np.broadcast_to((t >= p).astype(np.float32)[:, None], (c, GLA_DK)))
        same.append((blk[:, None] == blk[None, :]).astype(np.float32))
    same.append(2.0 * np.eye(c, dtype=np.float32))
    return (jnp.asarray(cum_f, BF16), jnp.asarray(cum_b, BF16),
            jnp.asarray(np.stack(upper + [1.0 - u for u in upper]), F32), jnp.asarray(np.stack(same), F32))


def _chunk_select(t_len):
    nc = t_len // GLA_CHUNK
    sel = (np.arange(t_len)[:, None] // GLA_CHUNK == np.arange(LANES)[None, :]).astype(np.float32)
    assert nc <= LANES
    return jnp.asarray(sel, BF16)


def _split_hi_lo(x):
    hi = x.astype(BF16)
    lo = (x - hi.astype(F32)).astype(BF16)
    return jnp.concatenate([hi, lo], axis=1)


def _pivot_rows(b, level):
    c = GLA_CHUNK
    n = c >> level
    if n >= 16:
        parts = [jnp.broadcast_to(b[s + n // 2:s + n // 2 + 1, :], (n, GLA_DK)) for s in range(0, c, n)]
        return parts[0] if len(parts) == 1 else jnp.concatenate(parts, axis=0)
    b3 = b.reshape(c // 8, 8, GLA_DK)
    if n == 8:
        return jnp.broadcast_to(b3[:, 4:5, :], b3.shape).reshape(c, GLA_DK)
    if n == 4:
        sub = lax.broadcasted_iota(jnp.int32, b3.shape, 1)
        lo = jnp.broadcast_to(b3[:, 2:3, :], b3.shape)
        hi = jnp.broadcast_to(b3[:, 6:7, :], b3.shape)
        return jnp.where(sub < 4, lo, hi).reshape(c, GLA_DK)
    row = lax.broadcasted_iota(jnp.int32, b.shape, 0)
    return jnp.where((row & 1) == 0, pltpu.roll(b, c - 1, 0), b)


def _gla_decay_stage(ci, q_ref, k_ref, bf_ref, bb_ref, up_ref, qc_ref, slot):
    qt_ref, kt_ref, kd_ref = slot
    c = GLA_CHUNK
    rows = pl.ds(pl.multiple_of(ci * c, c), c)
    q = q_ref[rows, :] * (GLA_DK ** -0.5)
    k = k_ref[rows, :]
    b_f = bf_ref[rows, :]
    b_b = bb_ref[rows, :]
    for level in range(GLA_LEVELS):
        wf = jnp.exp(-jnp.abs(b_f - _pivot_rows(b_f, level)))
        wb = jnp.exp(-jnp.abs(b_b - _pivot_rows(b_b, level)))
        up = up_ref[level]
        dn = up_ref[GLA_LEVELS + level]
        lv = slice(c * level, c * (level + 1))
        qt_ref[lv, :] = jnp.concatenate([q * (wf * up), q * (wb * dn)], axis=1).astype(BF16)
        kt_ref[lv, :] = jnp.concatenate([k * (wf * dn), k * (wb * up)], axis=1).astype(BF16)
    qc_ref[rows, :] = jnp.concatenate([q * jnp.exp(b_f), q * jnp.exp(b_b)], axis=1).astype(BF16)
    kd_ref[...] = jnp.concatenate([k * jnp.exp(b_f[c - 1:c, :] - b_f),
                                   k * jnp.exp(b_b[0:1, :] - b_b)], axis=1).astype(BF16)


def _gla_matmul_stage(ci, q_ref, k_ref, v_ref, same_ref, oi_ref, kv_ref, slot):
    qt_ref, kt_ref, kd_ref = slot
    c = GLA_CHUNK
    rows = pl.ds(pl.multiple_of(ci * c, c), c)
    q = (q_ref[rows, :] * (GLA_DK ** -0.5)).astype(BF16)
    k = k_ref[rows, :].astype(BF16)
    v = v_ref[rows, :].astype(BF16)
    att = _dot_nt(q, k) * same_ref[GLA_LEVELS]
    for level in range(GLA_LEVELS):
        lv = slice(c * level, c * (level + 1))
        att = att + _dot_nt(qt_ref[lv, :], kt_ref[lv, :]) * same_ref[level]
    oi_ref[rows, :] = _dot(att.astype(BF16), v)
    kv_ref[ci] = _dot_tn(kd_ref[...], v)


def _gla_kernel(*refs, t_len, has_state):
    if has_state:
        (q_ref, k_ref, v_ref, r_ref, zr_ref, wf_ref, wb_ref, bf_in_ref, bb_in_ref, gn_ref,
         cumf_ref, cumb_ref, up_ref, same_ref, sel_ref, s0f_ref, s0b_ref,
         o_ref, sf_ref, sb_ref, bf_ref, bb_ref, oi_ref, qc_ref, kv_ref, st_ref,
         qt0, kt0, kd0, qt1, kt1, kd1) = refs
    else:
        (q_ref, k_ref, v_ref, r_ref, zr_ref, wf_ref, wb_ref, bf_in_ref, bb_in_ref, gn_ref,
         cumf_ref, cumb_ref, up_ref, same_ref, sel_ref,
         o_ref, sf_ref, sb_ref, bf_ref, bb_ref, oi_ref, qc_ref, kv_ref, st_ref,
         qt0, kt0, kd0, qt1, kt1, kd1) = refs
    nc = t_len // GLA_CHUNK
    slots = ((qt0, kt0, kd0), (qt1, kt1, kd1))
    zr = zr_ref[...].astype(BF16)
    laf = _log_sigmoid(_dot(zr, wf_ref[...]) + bf_in_ref[...]) * (1.0 / GLA_GATE_TAU)
    lab = _log_sigmoid(_dot(zr, wb_ref[...]) + bb_in_ref[...]) * (1.0 / GLA_GATE_TAU)
    bf_ref[...] = laf
    bb_ref[...] = lab

    def chunk_totals(la):
        s = _dot_tn(_split_hi_lo(la), sel_ref[...])
        return jnp.exp(s[:GLA_DK, :] + s[GLA_DK:, :])

    tot_f = chunk_totals(laf)
    tot_b = chunk_totals(lab)

    def cumulate(ci, carry):
        rows = pl.ds(pl.multiple_of(ci * GLA_CHUNK, GLA_CHUNK), GLA_CHUNK)
        sf = _dot(cumf_ref[...], _split_hi_lo(bf_ref[rows, :]))
        sb = _dot(cumb_ref[...], _split_hi_lo(bb_ref[rows, :]))
        bf_ref[rows, :] = sf[:, :GLA_DK] + sf[:, GLA_DK:]
        bb_ref[rows, :] = sb[:, :GLA_DK] + sb[:, GLA_DK:]
        return carry

    lax.fori_loop(0, nc, cumulate, 0, unroll=GLA_UNROLL)

    decay = functools.partial(_gla_decay_stage, q_ref=q_ref, k_ref=k_ref, bf_ref=bf_ref, bb_ref=bb_ref,
                              up_ref=up_ref, qc_ref=qc_ref)
    matmuls = functools.partial(_gla_matmul_stage, q_ref=q_ref, k_ref=k_ref, v_ref=v_ref, same_ref=same_ref,
                                oi_ref=oi_ref, kv_ref=kv_ref)
    decay(0, slot=slots[0])
    for ci in range(nc):
        if ci + 1 < nc:
            decay(ci + 1, slot=slots[(ci + 1) % 2])
        matmuls(ci, slot=slots[ci % 2])

    s = s0f_ref[...] if has_state else jnp.zeros((GLA_DK, GLA_DV), F32)
    for ci in range(nc):
        st_ref[ci, 0:GLA_DK, :] = s.astype(BF16)
        s = tot_f[:, ci:ci + 1] * s + kv_ref[ci, 0:GLA_DK, :]
    sf_ref[...] = s
    s = s0b_ref[...] if has_state else jnp.zeros((GLA_DK, GLA_DV), F32)
    for ci in reversed(range(nc)):
        st_ref[ci, GLA_DK:2 * GLA_DK, :] = s.astype(BF16)
        s = tot_b[:, ci:ci + 1] * s + kv_ref[ci, GLA_DK:2 * GLA_DK, :]
    sb_ref[...] = s

    def finish(ci, carry):
        rows = pl.ds(pl.multiple_of(ci * GLA_CHUNK, GLA_CHUNK), GLA_CHUNK)
        o = oi_ref[rows, :] + _dot(qc_ref[rows, :], st_ref[ci])
        o = o * lax.rsqrt(jnp.mean(o * o, axis=-1, keepdims=True) + EPS) * gn_ref[...]
        r = r_ref[rows, :]
        o_ref[rows, :] = (o * (r * _sigmoid(r))).astype(o_ref.dtype)
        return carry

    lax.fori_loop(0, nc, finish, 0, unroll=GLA_UNROLL)


def _gla(za, wgf, wgb, b_gate_f, b_gate_b, gla_norm, layer, row0, n_seq, t_len, s0f=None, s0b=None):
    has_state = s0f is not None
    cum_f, cum_b, upper, same = _gla_consts()
    sel = _chunk_select(t_len)
    seq0 = row0 // t_len
    nc = t_len // GLA_CHUNK
    const = lambda shape: pl.BlockSpec(shape, lambda b, h: (0,) * len(shape))
    in_specs = [
        pl.BlockSpec((t_len, GLA_DK), lambda b, h: (seq0 + b, W_GQ // GLA_DK + h)),
        pl.BlockSpec((t_len, GLA_DK), lambda b, h: (seq0 + b, W_GK // GLA_DK + h)),
        pl.BlockSpec((t_len, GLA_DV), lambda b, h: (seq0 + b, W_GV // GLA_DV + h)),
        pl.BlockSpec((t_len, GLA_DV), lambda b, h: (seq0 + b, W_GR // GLA_DV + h)),
        pl.BlockSpec((t_len, LANES), lambda b, h: (seq0 + b, W_RANK // LANES)),
        pl.BlockSpec((None, LANES, GLA_DK), lambda b, h: (layer, 0, h)),
        pl.BlockSpec((None, LANES, GLA_DK), lambda b, h: (layer, 0, h)),
        pl.BlockSpec((None, 1, GLA_DK), lambda b, h: (layer, 0, h)),
        pl.BlockSpec((None, 1, GLA_DK), lambda b, h: (layer, 0, h)),
        pl.BlockSpec((None, 1, GLA_DV), lambda b, h: (layer, 0, 0)),
        const(cum_f.shape), const(cum_b.shape), const(upper.shape), const(same.shape), const(sel.shape),
    ]
    args = [za, za, za, za, za, wgf, wgb, b_gate_f.reshape(DEPTH, 1, GLA_K_DIM),
            b_gate_b.reshape(DEPTH, 1, GLA_K_DIM), gla_norm.reshape(DEPTH, 1, GLA_DV),
            cum_f, cum_b, upper, same, sel]
    if has_state:
        in_specs += [
            pl.BlockSpec((None, None, None, GLA_DK, GLA_DV), lambda b, h: (b, layer, h, 0, 0)),
            pl.BlockSpec((None, None, None, GLA_DK, GLA_DV), lambda b, h: (b, layer, h, 0, 0)),
        ]
        args += [s0f, s0b]
    return pl.pallas_call(
        functools.partial(_gla_kernel, t_len=t_len, has_state=has_state),
        grid=(n_seq, GLA_HEADS),
        in_specs=in_specs,
        out_specs=[
            pl.BlockSpec((t_len, GLA_DV), lambda b, h: (b, h)),
            pl.BlockSpec((None, None, GLA_DK, GLA_DV), lambda b, h: (b, h, 0, 0)),
            pl.BlockSpec((None, None, GLA_DK, GLA_DV), lambda b, h: (b, h, 0, 0)),
        ],
        out_shape=[
            jax.ShapeDtypeStruct((n_seq * t_len, GLA_V_DIM), BF16),
            jax.ShapeDtypeStruct((n_seq, GLA_HEADS, GLA_DK, GLA_DV), F32),
            jax.ShapeDtypeStruct((n_seq, GLA_HEADS, GLA_DK, GLA_DV), F32),
        ],
        scratch_shapes=[
            pltpu.VMEM((t_len, GLA_DK), F32), pltpu.VMEM((t_len, GLA_DK), F32),
            pltpu.VMEM((t_len, GLA_DV), F32), pltpu.VMEM((t_len, 2 * GLA_DK), BF16),
            pltpu.VMEM((nc, 2 * GLA_DK, GLA_DV), F32), pltpu.VMEM((nc, 2 * GLA_DK, GLA_DV), BF16),
        ] + 2 * [pltpu.VMEM((GLA_CHUNK * GLA_LEVELS, 2 * GLA_DK), BF16),
                 pltpu.VMEM((GLA_CHUNK * GLA_LEVELS, 2 * GLA_DK), BF16),
                 pltpu.VMEM((GLA_CHUNK, 2 * GLA_DK), BF16)],
        compiler_params=_params(("arbitrary", "arbitrary"), 40 << 20),
        name="gla_%d" % t_len,
    )(*args)


def _merge_kernel(fc_ref, fl_ref, ac_ref, al_ref, gc_ref, gl_ref, wa_ref, wb_ref, wc_ref,
                  ga_ref, gb_ref, gg_ref, o_ref, wa_s, wb_s, wc_s, *, ctx_tiles):
    i = pl.program_id(1)

    @pl.when(i == 0)
    def _():
        wa_s[...] = wa_ref[...].astype(BF16)
        wb_s[...] = wb_ref[...].astype(BF16)
        wc_s[...] = wc_ref[...].astype(BF16)

    def compute(f_ref, a_ref, g_ref):
        acc = _sigmoid(ga_ref[...].astype(F32)) * _dot(f_ref[...], wa_s[...])
        acc = acc + _sigmoid(gb_ref[...].astype(F32)) * _dot(a_ref[...], wb_s[...])
        acc = acc + _sigmoid(gg_ref[...].astype(F32)) * _dot(g_ref[...], wc_s[...])
        o_ref[...] = acc.astype(o_ref.dtype)

    @pl.when(i < ctx_tiles)
    def _():
        compute(fc_ref, ac_ref, gc_ref)

    @pl.when(i >= ctx_tiles)
    def _():
        compute(fl_ref, al_ref, gl_ref)


def _merge(fa, at, gl, w_fourier, w_attn, w_gla, zg, layer):
    tm, tn = 256, 1024
    k = F_DIM
    nj = D_MODEL // tn
    ctx_tiles = N_CTX // tm
    ctx = pl.BlockSpec((tm, k), lambda j, i: (jnp.minimum(i, ctx_tiles - 1), 0))
    lat = pl.BlockSpec((tm, k), lambda j, i: (jnp.maximum(i - ctx_tiles, 0), 0))
    wsp = pl.BlockSpec((None, k, tn), lambda j, i: (layer, 0, j))
    gate = lambda br: pl.BlockSpec((tm, tn), lambda j, i: (i, br * nj + j))
    return pl.pallas_call(
        functools.partial(_merge_kernel, ctx_tiles=ctx_tiles),
        grid=(nj, N_TOK // tm),
        in_specs=[ctx, lat, ctx, lat, ctx, lat, wsp, wsp, wsp, gate(0), gate(1), gate(2)],
        out_specs=pl.BlockSpec((tm, tn), lambda j, i: (i, j)),
        out_shape=jax.ShapeDtypeStruct((N_TOK, D_MODEL), BF16),
        scratch_shapes=[pltpu.VMEM((k, tn), BF16)] * 3,
        compiler_params=_params(("arbitrary", "arbitrary"),
                                2 * (6 * tm * k * 2 + 3 * k * tn * 4 + 3 * tm * tn * 4 + tm * tn * 2)
                                + 3 * k * tn * 2 + 4 * tm * tn * 4 + (8 << 20)),
        name="merge",
    )(fa[0], fa[1], at[0], at[1], gl[0], gl[1], w_fourier, w_attn, w_gla, zg, zg, zg)


def _convffn_up_kernel(h_ref, wg_ref, wv_ref, cwg_ref, cwv_ref, cbg_ref, cbv_ref, o_ref, wg_s, wv_s, *, tm):
    @pl.when(pl.program_id(1) == 0)
    def _():
        wg_s[...] = wg_ref[...].astype(BF16)
        wv_s[...] = wv_ref[...].astype(BF16)

    row0 = pl.program_id(1) * tm
    seq_len = jnp.where(row0 < N_CTX, SEQ, DEC_SEQ)
    h = h_ref[...]
    pos = lax.broadcasted_iota(jnp.int32, (tm, 1), 0) & (seq_len - 1)
    has_prev = (pos != 0).astype(F32)
    has_next = (pos != seq_len - 1).astype(F32)

    def conv(u, cw_ref, cb_ref):
        prev = pltpu.roll(u, 1, 0) * has_prev
        nxt = pltpu.roll(u, tm - 1, 0) * has_next
        return prev * cw_ref[0:1, :] + u * cw_ref[1:2, :] + nxt * cw_ref[2:3, :] + cb_ref[...]

    g = conv(_dot(h, wg_s[...]), cwg_ref, cbg_ref)
    val = conv(_dot(h, wv_s[...]), cwv_ref, cbv_ref)
    o_ref[...] = (g * _sigmoid(g) * val).astype(o_ref.dtype)


def _convffn_up(h, w_up, conv_w, conv_b, layer):
    tm, tn = ROW_TILE, 512
    k = D_MODEL
    nj = D_FF // tn
    cb = conv_b.reshape(DEPTH, 1, 2 * D_FF)
    return pl.pallas_call(
        functools.partial(_convffn_up_kernel, tm=tm),
        grid=(nj, N_TOK // tm),
        in_specs=[
            pl.BlockSpec((tm, k), lambda j, i: (i, 0)),
            pl.BlockSpec((None, k, tn), lambda j, i: (layer, 0, j)),
            pl.BlockSpec((None, k, tn), lambda j, i: (layer, 0, nj + j)),
            pl.BlockSpec((None, 3, tn), lambda j, i: (layer, 0, j)),
            pl.BlockSpec((None, 3, tn), lambda j, i: (layer, 0, nj + j)),
            pl.BlockSpec((None, 1, tn), lambda j, i: (layer, 0, j)),
            pl.BlockSpec((None, 1, tn), lambda j, i: (layer, 0, nj + j)),
        ],
        out_specs=pl.BlockSpec((tm, tn), lambda j, i: (i, j)),
        out_shape=jax.ShapeDtypeStruct((N_TOK, D_FF), BF16),
        scratch_shapes=[pltpu.VMEM((k, tn), BF16)] * 2,
        compiler_params=_params(("arbitrary", "arbitrary"),
                                2 * (tm * k * 2 + 2 * k * tn * 4 + tm * tn * 2) + 2 * k * tn * 2
                                + 8 * tm * tn * 4 + (8 << 20)),
        name="convffn_up",
    )(h, w_up, w_up, conv_w, conv_w, cb, cb)


def _final_norm_kernel(x_ref, g_ref, oc_ref, ol_ref, *, ctx_tiles):
    i = pl.program_id(0)
    x = x_ref[...]
    y = x * lax.rsqrt(jnp.mean(x * x, axis=-1, keepdims=True) + EPS) * g_ref[...]

    @pl.when(i < ctx_tiles)
    def _():
        oc_ref[...] = y

    @pl.when(i >= ctx_tiles)
    def _():
        ol_ref[...] = y


def _final_norm(x, g):
    tm = 512
    ctx_tiles = N_CTX // tm
    return pl.pallas_call(
        functools.partial(_final_norm_kernel, ctx_tiles=ctx_tiles),
        grid=(N_TOK // tm,),
        in_specs=[pl.BlockSpec((tm, D_MODEL), lambda m: (m, 0)),
                  pl.BlockSpec((1, D_MODEL), lambda m: (0, 0))],
        out_specs=[pl.BlockSpec((tm, D_MODEL), lambda m: (jnp.minimum(m, ctx_tiles - 1), 0)),
                   pl.BlockSpec((tm, D_MODEL), lambda m: (jnp.maximum(m - ctx_tiles, 0), 0))],
        out_shape=[jax.ShapeDtypeStruct((N_CTX, D_MODEL), F32), jax.ShapeDtypeStruct((N_LAT, D_MODEL), F32)],
        compiler_params=_params(("arbitrary",), 8 * tm * D_MODEL * 4 + (8 << 20)),
        name="final_norm",
    )(x, g.reshape(1, D_MODEL))


def _pad_gate_w(w_gate, row0):
    out = jnp.zeros((DEPTH, LANES, GLA_K_DIM), BF16)
    return out.at[:, row0:row0 + GLA_GATE_RANK, :].set(w_gate.astype(BF16))


def kernel(x_prompt, x_sample, cache_k, cache_v, state_gla_fwd, state_gla_bwd, c, c_ctx, w_ada, b_ada, norm1, w_in, q_norm, k_norm, w_fourier, w_attn, w_gate_f, b_gate_f, w_gate_b, b_gate_b, gla_norm, w_gla, w_out, norm2, w_up, conv_w, conv_b, w_down, final_norm):
    x = jnp.concatenate([x_prompt.reshape(N_CTX, D_MODEL), x_sample.reshape(N_LAT, D_MODEL)], axis=0)
    cvec = jnp.concatenate([c_ctx[None, :], c, jnp.zeros((8 - 1 - DEC_BATCH, D_MODEL), F32)], axis=0)
    mod = _ada(cvec, w_ada, b_ada).reshape(DEPTH, 8, 1, N_MOD * D_MODEL)
    w_t = jnp.swapaxes(w_in, 1, 2)
    wgf = _pad_gate_w(w_gate_f, 0)
    wgb = _pad_gate_w(w_gate_b, GLA_GATE_RANK)

    new_k, new_v, new_sf, new_sb = [], [], [], []
    for l in range(DEPTH):
        h = _modnorm(x, norm1, mod, l, 0, 1)
        za = _in_proj(h, w_t, l)
        zg = _gates_proj(h, w_t, l)
        fa = (_fnet(za, 0, BATCH, SEQ), _fnet(za, N_CTX, DEC_BATCH, DEC_SEQ))
        at_ctx, k_ctx, v_ctx = _attn_ctx(za, q_norm, k_norm, l)
        at = (at_ctx, _attn_lat(za, cache_k, cache_v, q_norm, k_norm, l))
        gl_ctx, sf, sb = _gla(za, wgf, wgb, b_gate_f, b_gate_b, gla_norm, l, 0, BATCH, SEQ)
        gl_lat, _, _ = _gla(za, wgf, wgb, b_gate_f, b_gate_b, gla_norm, l, N_CTX, DEC_BATCH, DEC_SEQ,
                            state_gla_fwd, state_gla_bwd)
        merged = _merge(fa, at, (gl_ctx, gl_lat), w_fourier, w_attn, w_gla, zg, l)
        x = _resid_proj(merged, w_out, x, mod, l, 2, ROW_TILE, 1024)
        h = _modnorm(x, norm2, mod, l, 3, 4)
        hmid = _convffn_up(h, w_up, conv_w, conv_b, l)
        x = _resid_proj(hmid, w_down, x, mod, l, 5, 512, 512)
        new_k.append(k_ctx)
        new_v.append(v_ctx)
        new_sf.append(sf)
        new_sb.append(sb)

    y_ctx, y_lat = _final_norm(x, final_norm)
    y_prompt = y_ctx.reshape(BATCH, SEQ, D_MODEL)
    y_sample = y_lat.reshape(DEC_BATCH, DEC_SEQ, D_MODEL)
    kv_shape = (BATCH, DEPTH, SEQ, N_KV_HEADS, HEAD_DIM)
    return (y_prompt, y_sample,
            jnp.stack(new_k, axis=1).reshape(kv_shape), jnp.stack(new_v, axis=1).reshape(kv_shape),
            jnp.stack(new_sf, axis=1), jnp.stack(new_sb, axis=1))
```

```python
import functools
import math

import numpy as np
import jax
import jax.numpy as jnp
from jax import lax
from jax.experimental import pallas as pl
from jax.experimental.pallas import tpu as pltpu

F32 = jnp.float32
BF16 = jnp.bfloat16

D_MODEL = 2048
BATCH = 16
SEQ = 256
DEPTH = 4
DEC_BATCH = 4
DEC_SEQ = 1024
PAST_LEN = 256
GRID_W = 64
HEAD_DIM = 128
N_Q_HEADS = 8
N_KV_HEADS = 2
N_GROUP = N_Q_HEADS // N_KV_HEADS
ATTN_DIM = N_Q_HEADS * HEAD_DIM
KV_DIM = N_KV_HEADS * HEAD_DIM
ROPE_THETA = 10000.0
ROPE_AXIS_DIM = HEAD_DIM // 2
F_GROUPS = 8
F_GROUP_DIM = 128
F_DIM = F_GROUPS * F_GROUP_DIM
GLA_HEADS = 4
GLA_DK = 128
GLA_DV = 256
GLA_K_DIM = GLA_HEADS * GLA_DK
GLA_V_DIM = GLA_HEADS * GLA_DV
GLA_GATE_RANK = 16
GLA_GATE_TAU = 16.0
D_FF = 5632
N_MOD = 6
N_BRANCH = 3
EPS = 1e-6

N_CTX = BATCH * SEQ
N_LAT = DEC_BATCH * DEC_SEQ
N_TOK = N_CTX + N_LAT

VMEM_CAP_BYTES = 60 * 1024 * 1024
LANES = 128

W_FIN = 0
W_Q = W_FIN + F_DIM
W_K = W_Q + ATTN_DIM
W_V = W_K + KV_DIM
W_GQ = W_V + KV_DIM
W_GK = W_GQ + GLA_K_DIM
W_GV = W_GK + GLA_K_DIM
W_GR = W_GV + GLA_V_DIM
W_RANK = W_GR + GLA_V_DIM
W_GATES = W_RANK + 2 * GLA_GATE_RANK
N_IN = W_GATES + N_BRANCH * D_MODEL

GLA_CHUNK = 64
GLA_LEVELS = 6
ROW_TILE = 1024
GLA_UNROLL = 4


def _params(semantics, vmem_bytes):
    return pltpu.CompilerParams(dimension_semantics=semantics,
                                vmem_limit_bytes=int(min(vmem_bytes, VMEM_CAP_BYTES)))


def _mod_row(row_start):
    return jnp.where(row_start < N_CTX, 0, 1 + (row_start - N_CTX) // DEC_SEQ)


def _sigmoid(x):
    return 1.0 / (1.0 + jnp.exp(-x))


def _log_sigmoid(x):
    return jnp.minimum(x, 0.0) - jnp.log(1.0 + jnp.exp(-jnp.abs(x)))


def _dot(a, b):
    return jnp.dot(a, b, preferred_element_type=F32)


def _dot_nt(a, b):
    return lax.dot_general(a, b, (((1,), (1,)), ((), ())), preferred_element_type=F32)


def _dot_tn(a, b):
    return lax.dot_general(a, b, (((0,), (0,)), ((), ())), preferred_element_type=F32)


def _ada_kernel(c_ref, w_ref, b_ref, o_ref):
    c = c_ref[...]
    a = (c * _sigmoid(c)).astype(BF16)
    o_ref[...] = _dot(a, w_ref[...].astype(BF16)) + b_ref[...]


def _ada(cvec, w_ada, b_ada):
    tn = 1024
    n = N_MOD * D_MODEL
    return pl.pallas_call(
        _ada_kernel,
        grid=(DEPTH, n // tn),
        in_specs=[
            pl.BlockSpec((8, D_MODEL), lambda l, j: (0, 0)),
            pl.BlockSpec((None, D_MODEL, tn), lambda l, j: (l, 0, j)),
            pl.BlockSpec((None, 1, tn), lambda l, j: (l, 0, j)),
        ],
        out_specs=pl.BlockSpec((None, 8, tn), lambda l, j: (l, 0, j)),
        out_shape=jax.ShapeDtypeStruct((DEPTH, 8, n), F32),
        compiler_params=_params(("arbitrary", "arbitrary"), 3 * D_MODEL * tn * 4 + (8 << 20)),
        name="ada_mod",
    )(cvec, w_ada, b_ada.reshape(DEPTH, 1, n))


def _modnorm_kernel(x_ref, g_ref, shift_ref, scale_ref, o_ref):
    x = x_ref[...]
    y = x * lax.rsqrt(jnp.mean(x * x, axis=-1, keepdims=True) + EPS) * g_ref[...]
    o_ref[...] = (y * (1.0 + scale_ref[...]) + shift_ref[...]).astype(o_ref.dtype)


def _modnorm(x, gains, mod, layer, shift_col, scale_col):
    tm = 512
    return pl.pallas_call(
        _modnorm_kernel,
        grid=(N_TOK // tm,),
        in_specs=[
            pl.BlockSpec((tm, D_MODEL), lambda m: (m, 0)),
            pl.BlockSpec((None, 1, D_MODEL), lambda m: (layer, 0, 0)),
            pl.BlockSpec((None, None, 1, D_MODEL), lambda m: (layer, _mod_row(m * tm), 0, shift_col)),
            pl.BlockSpec((None, None, 1, D_MODEL), lambda m: (layer, _mod_row(m * tm), 0, scale_col)),
        ],
        out_specs=pl.BlockSpec((tm, D_MODEL), lambda m: (m, 0)),
        out_shape=jax.ShapeDtypeStruct((N_TOK, D_MODEL), BF16),
        compiler_params=_params(("arbitrary",), 6 * tm * D_MODEL * 4 + (8 << 20)),
        name="modnorm",
    )(x, gains.reshape(DEPTH, 1, D_MODEL), mod, mod)


IN_TILE = 1024
ZA_COLS = 6 * IN_TILE
HALF_TILE = IN_TILE // 2
GATE_SHIFT = W_GATES - W_RANK


def _in_proj_kernel(a_ref, w_ref, o_ref, wbf_ref):
    @pl.when(pl.program_id(1) == 0)
    def _():
        wbf_ref[...] = w_ref[...].astype(BF16)

    o_ref[...] = _dot_nt(a_ref[...], wbf_ref[...])


def _in_proj(a, w_t, layer):
    m, k = a.shape
    tm, tn = ROW_TILE, IN_TILE
    return pl.pallas_call(
        _in_proj_kernel,
        grid=(ZA_COLS // tn, m // tm),
        in_specs=[
            pl.BlockSpec((tm, k), lambda j, i: (i, 0)),
            pl.BlockSpec((None, tn, k), lambda j, i: (layer, j, 0)),
        ],
        out_specs=pl.BlockSpec((tm, tn), lambda j, i: (i, j)),
        out_shape=jax.ShapeDtypeStruct((m, ZA_COLS), F32),
        scratch_shapes=[pltpu.VMEM((tn, k), BF16)],
        compiler_params=_params(("arbitrary", "arbitrary"),
                                2 * (tm * k * 2 + k * tn * 4 + tm * tn * 4) + k * tn * 2 + (8 << 20)),
        name="in_proj",
    )(a, w_t)


def _gates_kernel(a_ref, wa_ref, wb_ref, wx_ref, o_ref, wbf_ref):
    @pl.when(pl.program_id(1) == 0)
    def _():
        head = HALF_TILE - GATE_SHIFT
        wbf_ref[0:head, :] = wa_ref[GATE_SHIFT:, :].astype(BF16)
        wbf_ref[head:head + HALF_TILE, :] = wb_ref[...].astype(BF16)
        wbf_ref[head + HALF_TILE:, :] = wx_ref[...].astype(BF16)

    o_ref[...] = _dot_nt(a_ref[...], wbf_ref[...]).astype(o_ref.dtype)


def _gates_proj(a, w_t, layer):
    m, k = a.shape
    tm, tn = ROW_TILE, IN_TILE
    n = N_BRANCH * D_MODEL
    blk0 = W_RANK // HALF_TILE
    assert blk0 * HALF_TILE == W_RANK and HALF_TILE % GATE_SHIFT == 0
    xper = HALF_TILE // GATE_SHIFT
    return pl.pallas_call(
        _gates_kernel,
        grid=(n // tn, m // tm),
        in_specs=[
            pl.BlockSpec((tm, k), lambda j, i: (i, 0)),
            pl.BlockSpec((None, HALF_TILE, k), lambda j, i: (layer, blk0 + 2 * j, 0)),
            pl.BlockSpec((None, HALF_TILE, k), lambda j, i: (layer, blk0 + 2 * j + 1, 0)),
            pl.BlockSpec((None, GATE_SHIFT, k), lambda j, i: (layer, (blk0 + 2 * j + 2) * xper, 0)),
        ],
        out_specs=pl.BlockSpec((tm, tn), lambda j, i: (i, j)),
        out_shape=jax.ShapeDtypeStruct((m, n), BF16),
        scratch_shapes=[pltpu.VMEM((tn, k), BF16)],
        compiler_params=_params(("arbitrary", "arbitrary"),
                                2 * (tm * k * 2 + k * (tn + GATE_SHIFT) * 4 + tm * tn * 4) + k * tn * 2 + (8 << 20)),
        name="gates_proj",
    )(a, w_t, w_t, w_t)


def _resid_kernel(a_ref, w_ref, x_ref, gate_ref, o_ref, wbf_ref):
    @pl.when(pl.program_id(1) == 0)
    def _():
        wbf_ref[...] = w_ref[...].astype(BF16)

    o_ref[...] = x_ref[...] + gate_ref[...] * _dot(a_ref[...], wbf_ref[...])


def _resid_proj(a, w, x, mod, layer, gate_col, tm, tn):
    m, k = a.shape
    n = D_MODEL
    gate_blk = gate_col * (D_MODEL // tn)
    return pl.pallas_call(
        _resid_kernel,
        grid=(n // tn, m // tm),
        in_specs=[
            pl.BlockSpec((tm, k), lambda j, i: (i, 0)),
            pl.BlockSpec((None, k, tn), lambda j, i: (layer, 0, j)),
            pl.BlockSpec((tm, tn), lambda j, i: (i, j)),
            pl.BlockSpec((None, None, 1, tn), lambda j, i: (layer, _mod_row(i * tm), 0, gate_blk + j)),
        ],
        out_specs=pl.BlockSpec((tm, tn), lambda j, i: (i, j)),
        out_shape=jax.ShapeDtypeStruct((m, n), F32),
        scratch_shapes=[pltpu.VMEM((k, tn), BF16)],
        compiler_params=_params(("arbitrary", "arbitrary"),
                                2 * (tm * k * 2 + k * tn * 4 + 2 * tm * tn * 4) + k * tn * 2 + (8 << 20)),
        name="resid_proj",
    )(a, w, x, mod)


def _dft_consts(t_len):
    kc = np.arange(F_GROUP_DIM)
    ang_c = 2.0 * np.pi * ((kc[:, None] * kc[None, :]) % F_GROUP_DIM) / F_GROUP_DIM
    chan = np.concatenate([np.cos(ang_c), np.sin(ang_c)], axis=1)
    kt = np.arange(t_len)
    ang_t = 2.0 * np.pi * ((kt[:, None] * kt[None, :]) % t_len) / t_len
    pos = np.concatenate([np.cos(ang_t), -np.sin(ang_t)], axis=1)
    return jnp.asarray(chan, F32), jnp.asarray(pos, F32)


def _fnet_kernel(x_ref, chan_ref, pos_ref, o_ref, u_ref, *, t_len):
    @pl.when(pl.program_id(1) == 0)
    def _():
        chan = chan_ref[...].astype(BF16)
        for g in range(F_GROUPS):
            cols = slice(g * F_GROUP_DIM, (g + 1) * F_GROUP_DIM)
            cs = _dot(x_ref[:, cols].astype(BF16), chan)
            u_ref[0:t_len, cols] = cs[:, :F_GROUP_DIM].astype(BF16)
            u_ref[t_len:2 * t_len, cols] = cs[:, F_GROUP_DIM:].astype(BF16)

    y = _dot(pos_ref[...].astype(BF16), u_ref[...]) * (1.0 / math.sqrt(t_len * F_GROUP_DIM))
    o_ref[...] = y.astype(o_ref.dtype)


def _fnet(za, row0, n_seq, t_len):
    chan, pos = _dft_consts(t_len)
    tq = 256
    nq = t_len // tq
    seq_blk0 = row0 // t_len
    return pl.pallas_call(
        functools.partial(_fnet_kernel, t_len=t_len),
        grid=(n_seq, nq),
        in_specs=[
            pl.BlockSpec((t_len, F_DIM), lambda b, i: (seq_blk0 + b, W_FIN // F_DIM)),
            pl.BlockSpec((F_GROUP_DIM, 2 * F_GROUP_DIM), lambda b, i: (0, 0)),
            pl.BlockSpec((tq, 2 * t_len), lambda b, i: (i, 0)),
        ],
        out_specs=pl.BlockSpec((tq, F_DIM), lambda b, i: (b * nq + i, 0)),
        out_shape=jax.ShapeDtypeStruct((n_seq * t_len, F_DIM), BF16),
        scratch_shapes=[pltpu.VMEM((2 * t_len, F_DIM), BF16)],
        compiler_params=_params(("arbitrary", "arbitrary"),
                                2 * (t_len * F_DIM * 4 + tq * 2 * t_len * 4 + tq * F_DIM * 2)
                                + 2 * t_len * F_DIM * 2 + tq * 2 * t_len * 2 + tq * F_DIM * 8 + (8 << 20)),
        name="fnet_%d" % t_len,
    )(za, chan, pos)


def _head_rms(x, g):
    return x * lax.rsqrt(jnp.mean(x * x, axis=-1, keepdims=True) + EPS) * g


def _rope_tables(t_len):
    rows = t_len // GRID_W
    row = np.repeat(np.arange(rows, dtype=np.float64), GRID_W)
    col = np.tile(np.arange(GRID_W, dtype=np.float64), rows)
    inv = ROPE_THETA ** (-np.arange(0, ROPE_AXIS_DIM, 2, dtype=np.float64) / ROPE_AXIS_DIM)
    ar = row[:, None] * inv
    ac = col[:, None] * inv
    cos = np.concatenate([np.cos(ar), np.cos(ar), np.cos(ac), np.cos(ac)], axis=1)
    sin = np.concatenate([-np.sin(ar), np.sin(ar), -np.sin(ac), np.sin(ac)], axis=1)
    return jnp.asarray(cos, F32), jnp.asarray(sin, F32)


def _rope(x, cos, sin):
    lane = lax.broadcasted_iota(jnp.int32, x.shape, 1)
    low = (lane % ROPE_AXIS_DIM) < (ROPE_AXIS_DIM // 2)
    partner = jnp.where(low, pltpu.roll(x, HEAD_DIM - ROPE_AXIS_DIM // 2, 1), pltpu.roll(x, ROPE_AXIS_DIM // 2, 1))
    return x * cos + partner * sin


Q_SCALE = HEAD_DIM ** -0.5 * math.log2(math.e)


def _with_ones(v):
    lane = lax.broadcasted_iota(jnp.int32, v.shape, 1)
    return jnp.concatenate([v, jnp.where(lane == 0, 1.0, 0.0).astype(v.dtype)], axis=1)


def _softmax_pv(q, kb, vb1):
    s = _dot_nt((q * Q_SCALE).astype(BF16), kb)
    p = jnp.exp2(s - jnp.max(s, axis=-1, keepdims=True)).astype(BF16)
    ov = _dot(p, vb1)
    return ov[:, :HEAD_DIM] / ov[:, HEAD_DIM:HEAD_DIM + 1]


def _attn_ctx_kernel(q_ref, k_ref, v_ref, qn_ref, kn_ref, o_ref, ko_ref, vo_ref):
    k = _head_rms(k_ref[...], kn_ref[...])
    v = v_ref[...]
    ko_ref[...] = k
    vo_ref[...] = v
    kb = k.astype(BF16)
    vb = _with_ones(v.astype(BF16))
    for g in range(N_GROUP):
        cols = slice(g * HEAD_DIM, (g + 1) * HEAD_DIM)
        q = _head_rms(q_ref[:, cols], qn_ref[...])
        o_ref[:, cols] = _softmax_pv(q, kb, vb).astype(o_ref.dtype)


def _attn_ctx(za, q_norm, k_norm, layer):
    gw = N_GROUP * HEAD_DIM
    return pl.pallas_call(
        _attn_ctx_kernel,
        grid=(BATCH, N_KV_HEADS),
        in_specs=[
            pl.BlockSpec((SEQ, gw), lambda b, h: (b, W_Q // gw + h)),
            pl.BlockSpec((SEQ, HEAD_DIM), lambda b, h: (b, W_K // HEAD_DIM + h)),
            pl.BlockSpec((SEQ, HEAD_DIM), lambda b, h: (b, W_V // HEAD_DIM + h)),
            pl.BlockSpec((None, 1, HEAD_DIM), lambda b, h: (layer, 0, 0)),
            pl.BlockSpec((None, 1, HEAD_DIM), lambda b, h: (layer, 0, 0)),
        ],
        out_specs=[
            pl.BlockSpec((SEQ, gw), lambda b, h: (b, h)),
            pl.BlockSpec((None, SEQ, HEAD_DIM), lambda b, h: (b, 0, h)),
            pl.BlockSpec((None, SEQ, HEAD_DIM), lambda b, h: (b, 0, h)),
        ],
        out_shape=[
            jax.ShapeDtypeStruct((N_CTX, ATTN_DIM), BF16),
            jax.ShapeDtypeStruct((BATCH, SEQ, KV_DIM), F32),
            jax.ShapeDtypeStruct((BATCH, SEQ, KV_DIM), F32),
        ],
        compiler_params=_params(("arbitrary", "arbitrary"), 16 << 20),
        name="attn_ctx",
    )(za, za, za, q_norm.reshape(DEPTH, 1, HEAD_DIM), k_norm.reshape(DEPTH, 1, HEAD_DIM))


def _attn_lat_kernel(q_ref, k_ref, v_ref, ck_ref, cv_ref, qn_ref, kn_ref, cosq_ref, sinq_ref,
                     cosk_ref, sink_ref, o_ref, kb_ref, vb_ref):
    @pl.when(pl.program_id(2) == 0)
    def _():
        kb_ref[0:PAST_LEN, :] = ck_ref[...].astype(BF16)
        vb_ref[0:PAST_LEN, :] = _with_ones(cv_ref[...].astype(BF16))
        k = _rope(_head_rms(k_ref[...], kn_ref[...]), cosk_ref[...], sink_ref[...])
        kb_ref[PAST_LEN:, :] = k.astype(BF16)
        vb_ref[PAST_LEN:, :] = _with_ones(v_ref[...].astype(BF16))

    kb = kb_ref[...]
    vb = vb_ref[...]
    for g in range(N_GROUP):
        cols = slice(g * HEAD_DIM, (g + 1) * HEAD_DIM)
        q = _rope(_head_rms(q_ref[:, cols], qn_ref[...]), cosq_ref[...], sinq_ref[...])
        o_ref[:, cols] = _softmax_pv(q, kb, vb).astype(o_ref.dtype)


def _attn_lat(za, cache_k, cache_v, q_norm, k_norm, layer):
    gw = N_GROUP * HEAD_DIM
    tq = 256
    nq = DEC_SEQ // tq
    cos, sin = _rope_tables(DEC_SEQ)
    ck = cache_k.reshape(DEC_BATCH, DEPTH, PAST_LEN, KV_DIM)
    cv = cache_v.reshape(DEC_BATCH, DEPTH, PAST_LEN, KV_DIM)
    seq0 = N_CTX // DEC_SEQ
    tile0 = N_CTX // tq
    return pl.pallas_call(
        _attn_lat_kernel,
        grid=(DEC_BATCH, N_KV_HEADS, nq),
        in_specs=[
            pl.BlockSpec((tq, gw), lambda b, h, i: (tile0 + b * nq + i, W_Q // gw + h)),
            pl.BlockSpec((DEC_SEQ, HEAD_DIM), lambda b, h, i: (seq0 + b, W_K // HEAD_DIM + h)),
            pl.BlockSpec((DEC_SEQ, HEAD_DIM), lambda b, h, i: (seq0 + b, W_V // HEAD_DIM + h)),
            pl.BlockSpec((None, None, PAST_LEN, HEAD_DIM), lambda b, h, i: (b, layer, 0, h)),
            pl.BlockSpec((None, None, PAST_LEN, HEAD_DIM), lambda b, h, i: (b, layer, 0, h)),
            pl.BlockSpec((None, 1, HEAD_DIM), lambda b, h, i: (layer, 0, 0)),
            pl.BlockSpec((None, 1, HEAD_DIM), lambda b, h, i: (layer, 0, 0)),
            pl.BlockSpec((tq, HEAD_DIM), lambda b, h, i: (i, 0)),
            pl.BlockSpec((tq, HEAD_DIM), lambda b, h, i: (i, 0)),
            pl.BlockSpec((DEC_SEQ, HEAD_DIM), lambda b, h, i: (0, 0)),
            pl.BlockSpec((DEC_SEQ, HEAD_DIM), lambda b, h, i: (0, 0)),
        ],
        out_specs=pl.BlockSpec((tq, gw), lambda b, h, i: (b * nq + i, h)),
        out_shape=jax.ShapeDtypeStruct((N_LAT, ATTN_DIM), BF16),
        scratch_shapes=[pltpu.VMEM((PAST_LEN + DEC_SEQ, HEAD_DIM), BF16),
                        pltpu.VMEM((PAST_LEN + DEC_SEQ, 2 * HEAD_DIM), BF16)],
        compiler_params=_params(("arbitrary", "arbitrary", "arbitrary"), 32 << 20),
        name="attn_lat",
    )(za, za, za, ck, cv, q_norm.reshape(DEPTH, 1, HEAD_DIM), k_norm.reshape(DEPTH, 1, HEAD_DIM),
      cos, sin, cos, sin)


def _gla_consts():
    c = GLA_CHUNK
    t = np.arange(c)
    cum_f = (t[None, :] <= t[:, None]).astype(np.float32)
    cum_b = (t[None, :] >= t[:, None]).astype(np.float32)
    upper, same = [], []
    for level in range(GLA_LEVELS):
        n = c >> level
        blk = t // n
        p = blk * n + n // 2
        upper.append(np.broadcast_to((t >= p).astype(np.float32)[:, None], (c, GLA_DK)))
        same.append((blk[:, None] == blk[None, :]).astype(np.float32))
    same.append(2.0 * np.eye(c, dtype=np.float32))
    return (jnp.asarray(cum_f, BF16), jnp.asarray(cum_b, BF16),
            jnp.asarray(np.stack(upper + [1.0 - u for u in upper]), F32), jnp.asarray(np.stack(same), F32))


def _chunk_select(t_len):
    nc = t_len // GLA_CHUNK
    sel = (np.arange(t_len)[:, None] // GLA_CHUNK == np.arange(LANES)[None, :]).astype(np.float32)
    assert nc <= LANES
    return jnp.asarray(sel, BF16)


def _split_hi_lo(x):
    hi = x.astype(BF16)
    lo = (x - hi.astype(F32)).astype(BF16)
    return jnp.concatenate([hi, lo], axis=1)


def _pivot_rows(b, level):
    c = GLA_CHUNK
    n = c >> level
    if n >= 16:
        parts = [jnp.broadcast_to(b[s + n // 2:s + n // 2 + 1, :], (n, GLA_DK)) for s in range(0, c, n)]
        return parts[0] if len(parts) == 1 else jnp.concatenate(parts, axis=0)
    b3 = b.reshape(c // 8, 8, GLA_DK)
    if n == 8:
        return jnp.broadcast_to(b3[:, 4:5, :], b3.shape).reshape(c, GLA_DK)
    if n == 4:
        sub = lax.broadcasted_iota(jnp.int32, b3.shape, 1)
        lo = jnp.broadcast_to(b3[:, 2:3, :], b3.shape)
        hi = jnp.broadcast_to(b3[:, 6:7, :], b3.shape)
        return jnp.where(sub < 4, lo, hi).reshape(c, GLA_DK)
    row = lax.broadcasted_iota(jnp.int32, b.shape, 0)
    return jnp.where((row & 1) == 0, pltpu.roll(b, c - 1, 0), b)


def _gla_decay_stage(ci, q_ref, k_ref, bf_ref, bb_ref, up_ref, qc_ref, slot):
    qt_ref, kt_ref, kd_ref = slot
    c = GLA_CHUNK
    rows = pl.ds(pl.multiple_of(ci * c, c), c)
    q = q_ref[rows, :] * (GLA_DK ** -0.5)
    k = k_ref[rows, :]
    b_f = bf_ref[rows, :]
    b_b = bb_ref[rows, :]
    for level in range(GLA_LEVELS):
        wf = jnp.exp(-jnp.abs(b_f - _pivot_rows(b_f, level)))
        wb = jnp.exp(-jnp.abs(b_b - _pivot_rows(b_b, level)))
        up = up_ref[level]
        dn = up_ref[GLA_LEVELS + level]
        lv = slice(c * level, c * (level + 1))
        qt_ref[lv, :] = jnp.concatenate([q * (wf * up), q * (wb * dn)], axis=1).astype(BF16)
        kt_ref[lv, :] = jnp.concatenate([k * (wf * dn), k * (wb * up)], axis=1).astype(BF16)
    qc_ref[rows, :] = jnp.concatenate([q * jnp.exp(b_f), q * jnp.exp(b_b)], axis=1).astype(BF16)
    kd_ref[...] = jnp.concatenate([k * jnp.exp(b_f[c - 1:c, :] - b_f),
                                   k * jnp.exp(b_b[0:1, :] - b_b)], axis=1).astype(BF16)


def _gla_matmul_stage(ci, q_ref, k_ref, v_ref, same_ref, oi_ref, kv_ref, slot):
    qt_ref, kt_ref, kd_ref = slot
    c = GLA_CHUNK
    rows = pl.ds(pl.multiple_of(ci * c, c), c)
    q = (q_ref[rows, :] * (GLA_DK ** -0.5)).astype(BF16)
    k = k_ref[rows, :].astype(BF16)
    v = v_ref[rows, :].astype(BF16)
    att = _dot_nt(q, k) * same_ref[GLA_LEVELS]
    for level in range(GLA_LEVELS):
        lv = slice(c * level, c * (level + 1))
        att = att + _dot_nt(qt_ref[lv, :], kt_ref[lv, :]) * same_ref[level]
    oi_ref[rows, :] = _dot(att.astype(BF16), v)
    kv_ref[ci] = _dot_tn(kd_ref[...], v)


def _gla_kernel(*refs, t_len, has_state):
    if has_state:
        (q_ref, k_ref, v_ref, r_ref, zr_ref, wf_ref, wb_ref, bf_in_ref, bb_in_ref, gn_ref,
         cumf_ref, cumb_ref, up_ref, same_ref, sel_ref, s0f_ref, s0b_ref,
         o_ref, sf_ref, sb_ref, bf_ref, bb_ref, oi_ref, qc_ref, kv_ref, st_ref,
         qt0, kt0, kd0, qt1, kt1, kd1) = refs
    else:
        (q_ref, k_ref, v_ref, r_ref, zr_ref, wf_ref, wb_ref, bf_in_ref, bb_in_ref, gn_ref,
         cumf_ref, cumb_ref, up_ref, same_ref, sel_ref,
         o_ref, sf_ref, sb_ref, bf_ref, bb_ref, oi_ref, qc_ref, kv_ref, st_ref,
         qt0, kt0, kd0, qt1, kt1, kd1) = refs
    nc = t_len // GLA_CHUNK
    slots = ((qt0, kt0, kd0), (qt1, kt1, kd1))
    zr = zr_ref[...].astype(BF16)
    laf = _log_sigmoid(_dot(zr, wf_ref[...]) + bf_in_ref[...]) * (1.0 / GLA_GATE_TAU)
    lab = _log_sigmoid(_dot(zr, wb_ref[...]) + bb_in_ref[...]) * (1.0 / GLA_GATE_TAU)
    bf_ref[...] = laf
    bb_ref[...] = lab

    def chunk_totals(la):
        s = _dot_tn(_split_hi_lo(la), sel_ref[...])
        return jnp.exp(s[:GLA_DK, :] + s[GLA_DK:, :])

    tot_f = chunk_totals(laf)
    tot_b = chunk_totals(lab)

    def cumulate(ci, carry):
        rows = pl.ds(pl.multiple_of(ci * GLA_CHUNK, GLA_CHUNK), GLA_CHUNK)
        sf = _dot(cumf_ref[...], _split_hi_lo(bf_ref[rows, :]))
        sb = _dot(cumb_ref[...], _split_hi_lo(bb_ref[rows, :]))
        bf_ref[rows, :] = sf[:, :GLA_DK] + sf[:, GLA_DK:]
        bb_ref[rows, :] = sb[:, :GLA_DK] + sb[:, GLA_DK:]
        return carry

    lax.fori_loop(0, nc, cumulate, 0, unroll=GLA_UNROLL)

    decay = functools.partial(_gla_decay_stage, q_ref=q_ref, k_ref=k_ref, bf_ref=bf_ref, bb_ref=bb_ref,
                              up_ref=up_ref, qc_ref=qc_ref)
    matmuls = functools.partial(_gla_matmul_stage, q_ref=q_ref, k_ref=k_ref, v_ref=v_ref, same_ref=same_ref,
                                oi_ref=oi_ref, kv_ref=kv_ref)
    decay(0, slot=slots[0])
    for ci in range(nc):
        if ci + 1 < nc:
            decay(ci + 1, slot=slots[(ci + 1) % 2])
        matmuls(ci, slot=slots[ci % 2])

    s = s0f_ref[...] if has_state else jnp.zeros((GLA_DK, GLA_DV), F32)
    for ci in range(nc):
        st_ref[ci, 0:GLA_DK, :] = s.astype(BF16)
        s = tot_f[:, ci:ci + 1] * s + kv_ref[ci, 0:GLA_DK, :]
    sf_ref[...] = s
    s = s0b_ref[...] if has_state else jnp.zeros((GLA_DK, GLA_DV), F32)
    for ci in reversed(range(nc)):
        st_ref[ci, GLA_DK:2 * GLA_DK, :] = s.astype(BF16)
        s = tot_b[:, ci:ci + 1] * s + kv_ref[ci, GLA_DK:2 * GLA_DK, :]
    sb_ref[...] = s

    def finish(ci, carry):
        rows = pl.ds(pl.multiple_of(ci * GLA_CHUNK, GLA_CHUNK), GLA_CHUNK)
        o = oi_ref[rows, :] + _dot(qc_ref[rows, :], st_ref[ci])
        o = o * lax.rsqrt(jnp.mean(o * o, axis=-1, keepdims=True) + EPS) * gn_ref[...]
        r = r_ref[rows, :]
        o_ref[rows, :] = (o * (r * _sigmoid(r))).astype(o_ref.dtype)
        return carry

    lax.fori_loop(0, nc, finish, 0, unroll=GLA_UNROLL)


def _gla(za, wgf, wgb, b_gate_f, b_gate_b, gla_norm, layer, row0, n_seq, t_len, s0f=None, s0b=None):
    has_state = s0f is not None
    cum_f, cum_b, upper, same = _gla_consts()
    sel = _chunk_select(t_len)
    seq0 = row0 // t_len
    nc = t_len // GLA_CHUNK
    const = lambda shape: pl.BlockSpec(shape, lambda b, h: (0,) * len(shape))
    in_specs = [
        pl.BlockSpec((t_len, GLA_DK), lambda b, h: (seq0 + b, W_GQ // GLA_DK + h)),
        pl.BlockSpec((t_len, GLA_DK), lambda b, h: (seq0 + b, W_GK // GLA_DK + h)),
        pl.BlockSpec((t_len, GLA_DV), lambda b, h: (seq0 + b, W_GV // GLA_DV + h)),
        pl.BlockSpec((t_len, GLA_DV), lambda b, h: (seq0 + b, W_GR // GLA_DV + h)),
        pl.BlockSpec((t_len, LANES), lambda b, h: (seq0 + b, W_RANK // LANES)),
        pl.BlockSpec((None, LANES, GLA_DK), lambda b, h: (layer, 0, h)),
        pl.BlockSpec((None, LANES, GLA_DK), lambda b, h: (layer, 0, h)),
        pl.BlockSpec((None, 1, GLA_DK), lambda b, h: (layer, 0, h)),
        pl.BlockSpec((None, 1, GLA_DK), lambda b, h: (layer, 0, h)),
        pl.BlockSpec((None, 1, GLA_DV), lambda b, h: (layer, 0, 0)),
        const(cum_f.shape), const(cum_b.shape), const(upper.shape), const(same.shape), const(sel.shape),
    ]
    args = [za, za, za, za, za, wgf, wgb, b_gate_f.reshape(DEPTH, 1, GLA_K_DIM),
            b_gate_b.reshape(DEPTH, 1, GLA_K_DIM), gla_norm.reshape(DEPTH, 1, GLA_DV),
            cum_f, cum_b, upper, same, sel]
    if has_state:
        in_specs += [
            pl.BlockSpec((None, None, None, GLA_DK, GLA_DV), lambda b, h: (b, layer, h, 0, 0)),
            pl.BlockSpec((None, None, None, GLA_DK, GLA_DV), lambda b, h: (b, layer, h, 0, 0)),
        ]
        args += [s0f, s0b]
    return pl.pallas_call(
        functools.partial(_gla_kernel, t_len=t_len, has_state=has_state),
        grid=(n_seq, GLA_HEADS),
        in_specs=in_specs,
        out_specs=[
            pl.BlockSpec((t_len, GLA_DV), lambda b, h: (b, h)),
            pl.BlockSpec((None, None, GLA_DK, GLA_DV), lambda b, h: (b, h, 0, 0)),
            pl.BlockSpec((None, None, GLA_DK, GLA_DV), lambda b, h: (b, h, 0, 0)),
        ],
        out_shape=[
            jax.ShapeDtypeStruct((n_seq * t_len, GLA_V_DIM), BF16),
            jax.ShapeDtypeStruct((n_seq, GLA_HEADS, GLA_DK, GLA_DV), F32),
            jax.ShapeDtypeStruct((n_seq, GLA_HEADS, GLA_DK, GLA_DV), F32),
        ],
        scratch_shapes=[
            pltpu.VMEM((t_len, GLA_DK), F32), pltpu.VMEM((t_len, GLA_DK), F32),
            pltpu.VMEM((t_len, GLA_DV), F32), pltpu.VMEM((t_len, 2 * GLA_DK), BF16),
            pltpu.VMEM((nc, 2 * GLA_DK, GLA_DV), F32), pltpu.VMEM((nc, 2 * GLA_DK, GLA_DV), BF16),
        ] + 2 * [pltpu.VMEM((GLA_CHUNK * GLA_LEVELS, 2 * GLA_DK), BF16),
                 pltpu.VMEM((GLA_CHUNK * GLA_LEVELS, 2 * GLA_DK), BF16),
                 pltpu.VMEM((GLA_CHUNK, 2 * GLA_DK), BF16)],
        compiler_params=_params(("arbitrary", "arbitrary"), 40 << 20),
        name="gla_%d" % t_len,
    )(*args)


def _merge_kernel(fc_ref, fl_ref, ac_ref, al_ref, gc_ref, gl_ref, wa_ref, wb_ref, wc_ref,
                  ga_ref, gb_ref, gg_ref, o_ref, wa_s, wb_s, wc_s, *, ctx_tiles):
    i = pl.program_id(1)

    @pl.when(i == 0)
    def _():
        wa_s[...] = wa_ref[...].astype(BF16)
        wb_s[...] = wb_ref[...].astype(BF16)
        wc_s[...] = wc_ref[...].astype(BF16)

    def compute(f_ref, a_ref, g_ref):
        acc = _sigmoid(ga_ref[...].astype(F32)) * _dot(f_ref[...], wa_s[...])
        acc = acc + _sigmoid(gb_ref[...].astype(F32)) * _dot(a_ref[...], wb_s[...])
        acc = acc + _sigmoid(gg_ref[...].astype(F32)) * _dot(g_ref[...], wc_s[...])
        o_ref[...] = acc.astype(o_ref.dtype)

    @pl.when(i < ctx_tiles)
    def _():
        compute(fc_ref, ac_ref, gc_ref)

    @pl.when(i >= ctx_tiles)
    def _():
        compute(fl_ref, al_ref, gl_ref)


def _merge(fa, at, gl, w_fourier, w_attn, w_gla, zg, layer):
    tm, tn = 256, 1024
    k = F_DIM
    nj = D_MODEL // tn
    ctx_tiles = N_CTX // tm
    ctx = pl.BlockSpec((tm, k), lambda j, i: (jnp.minimum(i, ctx_tiles - 1), 0))
    lat = pl.BlockSpec((tm, k), lambda j, i: (jnp.maximum(i - ctx_tiles, 0), 0))
    wsp = pl.BlockSpec((None, k, tn), lambda j, i: (layer, 0, j))
    gate = lambda br: pl.BlockSpec((tm, tn), lambda j, i: (i, br * nj + j))
    return pl.pallas_call(
        functools.partial(_merge_kernel, ctx_tiles=ctx_tiles),
        grid=(nj, N_TOK // tm),
        in_specs=[ctx, lat, ctx, lat, ctx, lat, wsp, wsp, wsp, gate(0), gate(1), gate(2)],
        out_specs=pl.BlockSpec((tm, tn), lambda j, i: (i, j)),
        out_shape=jax.ShapeDtypeStruct((N_TOK, D_MODEL), BF16),
        scratch_shapes=[pltpu.VMEM((k, tn), BF16)] * 3,
        compiler_params=_params(("arbitrary", "arbitrary"),
                                2 * (6 * tm * k * 2 + 3 * k * tn * 4 + 3 * tm * tn * 4 + tm * tn * 2)
                                + 3 * k * tn * 2 + 4 * tm * tn * 4 + (8 << 20)),
        name="merge",
    )(fa[0], fa[1], at[0], at[1], gl[0], gl[1], w_fourier, w_attn, w_gla, zg, zg, zg)


def _convffn_up_kernel(h_ref, wg_ref, wv_ref, cwg_ref, cwv_ref, cbg_ref, cbv_ref, o_ref, wg_s, wv_s, *, tm):
    @pl.when(pl.program_id(1) == 0)
    def _():
        wg_s[...] = wg_ref[...].astype(BF16)
        wv_s[...] = wv_ref[...].astype(BF16)

    row0 = pl.program_id(1) * tm
    seq_len = jnp.where(row0 < N_CTX, SEQ, DEC_SEQ)
    h = h_ref[...]
    pos = lax.broadcasted_iota(jnp.int32, (tm, 1), 0) & (seq_len - 1)
    has_prev = (pos != 0).astype(F32)
    has_next = (pos != seq_len - 1).astype(F32)

    def conv(u, cw_ref, cb_ref):
        prev = pltpu.roll(u, 1, 0) * has_prev
        nxt = pltpu.roll(u, tm - 1, 0) * has_next
        return prev * cw_ref[0:1, :] + u * cw_ref[1:2, :] + nxt * cw_ref[2:3, :] + cb_ref[...]

    g = conv(_dot(h, wg_s[...]), cwg_ref, cbg_ref)
    val = conv(_dot(h, wv_s[...]), cwv_ref, cbv_ref)
    o_ref[...] = (g * _sigmoid(g) * val).astype(o_ref.dtype)


def _convffn_up(h, w_up, conv_w, conv_b, layer):
    tm, tn = ROW_TILE, 512
    k = D_MODEL
    nj = D_FF // tn
    cb = conv_b.reshape(DEPTH, 1, 2 * D_FF)
    return pl.pallas_call(
        functools.partial(_convffn_up_kernel, tm=tm),
        grid=(nj, N_TOK // tm),
        in_specs=[
            pl.BlockSpec((tm, k), lambda j, i: (i, 0)),
            pl.BlockSpec((None, k, tn), lambda j, i: (layer, 0, j)),
            pl.BlockSpec((None, k, tn), lambda j, i: (layer, 0, nj + j)),
            pl.BlockSpec((None, 3, tn), lambda j, i: (layer, 0, j)),
            pl.BlockSpec((None, 3, tn), lambda j, i: (layer, 0, nj + j)),
            pl.BlockSpec((None, 1, tn), lambda j, i: (layer, 0, j)),
            pl.BlockSpec((None, 1, tn), lambda j, i: (layer, 0, nj + j)),
        ],
        out_specs=pl.BlockSpec((tm, tn), lambda j, i: (i, j)),
        out_shape=jax.ShapeDtypeStruct((N_TOK, D_FF), BF16),
        scratch_shapes=[pltpu.VMEM((k, tn), BF16)] * 2,
        compiler_params=_params(("arbitrary", "arbitrary"),
                                2 * (tm * k * 2 + 2 * k * tn * 4 + tm * tn * 2) + 2 * k * tn * 2
                                + 8 * tm * tn * 4 + (8 << 20)),
        name="convffn_up",
    )(h, w_up, w_up, conv_w, conv_w, cb, cb)


def _final_norm_kernel(x_ref, g_ref, oc_ref, ol_ref, *, ctx_tiles):
    i = pl.program_id(0)
    x = x_ref[...]
    y = x * lax.rsqrt(jnp.mean(x * x, axis=-1, keepdims=True) + EPS) * g_ref[...]

    @pl.when(i < ctx_tiles)
    def _():
        oc_ref[...] = y

    @pl.when(i >= ctx_tiles)
    def _():
        ol_ref[...] = y


def _final_norm(x, g):
    tm = 512
    ctx_tiles = N_CTX // tm
    return pl.pallas_call(
        functools.partial(_final_norm_kernel, ctx_tiles=ctx_tiles),
        grid=(N_TOK // tm,),
        in_specs=[pl.BlockSpec((tm, D_MODEL), lambda m: (m, 0)),
                  pl.BlockSpec((1, D_MODEL), lambda m: (0, 0))],
        out_specs=[pl.BlockSpec((tm, D_MODEL), lambda m: (jnp.minimum(m, ctx_tiles - 1), 0)),
                   pl.BlockSpec((tm, D_MODEL), lambda m: (jnp.maximum(m - ctx_tiles, 0), 0))],
        out_shape=[jax.ShapeDtypeStruct((N_CTX, D_MODEL), F32), jax.ShapeDtypeStruct((N_LAT, D_MODEL), F32)],
        compiler_params=_params(("arbitrary",), 8 * tm * D_MODEL * 4 + (8 << 20)),
        name="final_norm",
    )(x, g.reshape(1, D_MODEL))


def _pad_gate_w(w_gate, row0):
    out = jnp.zeros((DEPTH, LANES, GLA_K_DIM), BF16)
    return out.at[:, row0:row0 + GLA_GATE_RANK, :].set(w_gate.astype(BF16))


def kernel(x_prompt, x_sample, cache_k, cache_v, state_gla_fwd, state_gla_bwd, c, c_ctx, w_ada, b_ada, norm1, w_in, q_norm, k_norm, w_fourier, w_attn, w_gate_f, b_gate_f, w_gate_b, b_gate_b, gla_norm, w_gla, w_out, norm2, w_up, conv_w, conv_b, w_down, final_norm):
    x = jnp.concatenate([x_prompt.reshape(N_CTX, D_MODEL), x_sample.reshape(N_LAT, D_MODEL)], axis=0)
    cvec = jnp.concatenate([c_ctx[None, :], c, jnp.zeros((8 - 1 - DEC_BATCH, D_MODEL), F32)], axis=0)
    mod = _ada(cvec, w_ada, b_ada).reshape(DEPTH, 8, 1, N_MOD * D_MODEL)
    w_t = jnp.swapaxes(w_in, 1, 2)
    wgf = _pad_gate_w(w_gate_f, 0)
    wgb = _pad_gate_w(w_gate_b, GLA_GATE_RANK)

    new_k, new_v, new_sf, new_sb = [], [], [], []
    for l in range(DEPTH):
        h = _modnorm(x, norm1, mod, l, 0, 1)
        za = _in_proj(h, w_t, l)
        zg = _gates_proj(h, w_t, l)
        fa = (_fnet(za, 0, BATCH, SEQ), _fnet(za, N_CTX, DEC_BATCH, DEC_SEQ))
        at_ctx, k_ctx, v_ctx = _attn_ctx(za, q_norm, k_norm, l)
        at = (at_ctx, _attn_lat(za, cache_k, cache_v, q_norm, k_norm, l))
        gl_ctx, sf, sb = _gla(za, wgf, wgb, b_gate_f, b_gate_b, gla_norm, l, 0, BATCH, SEQ)
        gl_lat, _, _ = _gla(za, wgf, wgb, b_gate_f, b_gate_b, gla_norm, l, N_CTX, DEC_BATCH, DEC_SEQ,
                            state_gla_fwd, state_gla_bwd)
        merged = _merge(fa, at, (gl_ctx, gl_lat), w_fourier, w_attn, w_gla, zg, l)
        x = _resid_proj(merged, w_out, x, mod, l, 2, ROW_TILE, 1024)
        h = _modnorm(x, norm2, mod, l, 3, 4)
        hmid = _convffn_up(h, w_up, conv_w, conv_b, l)
        x = _resid_proj(hmid, w_down, x, mod, l, 5, 512, 512)
        new_k.append(k_ctx)
        new_v.append(v_ctx)
        new_sf.append(sf)
        new_sb.append(sb)

    y_ctx, y_lat = _final_norm(x, final_norm)
    y_prompt = y_ctx.reshape(BATCH, SEQ, D_MODEL)
    y_sample = y_lat.reshape(DEC_BATCH, DEC_SEQ, D_MODEL)
    kv_shape = (BATCH, DEPTH, SEQ, N_KV_HEADS, HEAD_DIM)
    return (y_prompt, y_sample,
            jnp.stack(new_k, axis=1).reshape(kv_shape), jnp.stack(new_v, axis=1).reshape(kv_shape),
            jnp.stack(new_sf, axis=1), jnp.stack(new_sb, axis=1))
```

```python
import functools
import math

import numpy as np
import jax
import jax.numpy as jnp
from jax import lax
from jax.experimental import pallas as pl
from jax.experimental.pallas import tpu as pltpu

F32 = jnp.float32
BF16 = jnp.bfloat16

D_MODEL = 2048
BATCH = 16
SEQ = 256
DEPTH = 4
DEC_BATCH = 4
DEC_SEQ = 1024
PAST_LEN = 256
GRID_W = 64
HEAD_DIM = 128
N_Q_HEADS = 8
N_KV_HEADS = 2
N_GROUP = N_Q_HEADS // N_KV_HEADS
ATTN_DIM = N_Q_HEADS * HEAD_DIM
KV_DIM = N_KV_HEADS * HEAD_DIM
ROPE_THETA = 10000.0
ROPE_AXIS_DIM = HEAD_DIM // 2
F_GROUPS = 8
F_GROUP_DIM = 128
F_DIM = F_GROUPS * F_GROUP_DIM
GLA_HEADS = 4
GLA_DK = 128
GLA_DV = 256
GLA_K_DIM = GLA_HEADS * GLA_DK
GLA_V_DIM = GLA_HEADS * GLA_DV
GLA_GATE_RANK = 16
GLA_GATE_TAU = 16.0
D_FF = 5632
N_MOD = 6
N_BRANCH = 3
EPS = 1e-6

N_CTX = BATCH * SEQ
N_LAT = DEC_BATCH * DEC_SEQ
N_TOK = N_CTX + N_LAT

VMEM_CAP_BYTES = 60 * 1024 * 1024
LANES = 128

W_FIN = 0
W_Q = W_FIN + F_DIM
W_K = W_Q + ATTN_DIM
W_V = W_K + KV_DIM
W_GQ = W_V + KV_DIM
W_GK = W_GQ + GLA_K_DIM
W_GV = W_GK + GLA_K_DIM
W_GR = W_GV + GLA_V_DIM
W_RANK = W_GR + GLA_V_DIM
W_GATES = W_RANK + 2 * GLA_GATE_RANK
N_IN = W_GATES + N_BRANCH * D_MODEL

GLA_CHUNK = 64
GLA_LEVELS = 6
ROW_TILE = 1024


def _params(semantics, vmem_bytes):
    return pltpu.CompilerParams(dimension_semantics=semantics,
                                vmem_limit_bytes=int(min(vmem_bytes, VMEM_CAP_BYTES)))


def _mod_row(row_start):
    return jnp.where(row_start < N_CTX, 0, 1 + (row_start - N_CTX) // DEC_SEQ)


def _sigmoid(x):
    return 1.0 / (1.0 + jnp.exp(-x))


def _log_sigmoid(x):
    return jnp.minimum(x, 0.0) - jnp.log(1.0 + jnp.exp(-jnp.abs(x)))


def _dot(a, b):
    return jnp.dot(a, b, preferred_element_type=F32)


def _dot_nt(a, b):
    return lax.dot_general(a, b, (((1,), (1,)), ((), ())), preferred_element_type=F32)


def _dot_tn(a, b):
    return lax.dot_general(a, b, (((0,), (0,)), ((), ())), preferred_element_type=F32)


def _ada_kernel(c_ref, w_ref, b_ref, o_ref):
    c = c_ref[...]
    a = (c * _sigmoid(c)).astype(BF16)
    o_ref[...] = _dot(a, w_ref[...].astype(BF16)) + b_ref[...]


def _ada(cvec, w_ada, b_ada):
    tn = 1024
    n = N_MOD * D_MODEL
    return pl.pallas_call(
        _ada_kernel,
        grid=(DEPTH, n // tn),
        in_specs=[
            pl.BlockSpec((8, D_MODEL), lambda l, j: (0, 0)),
            pl.BlockSpec((None, D_MODEL, tn), lambda l, j: (l, 0, j)),
            pl.BlockSpec((None, 1, tn), lambda l, j: (l, 0, j)),
        ],
        out_specs=pl.BlockSpec((None, 8, tn), lambda l, j: (l, 0, j)),
        out_shape=jax.ShapeDtypeStruct((DEPTH, 8, n), F32),
        compiler_params=_params(("arbitrary", "arbitrary"), 3 * D_MODEL * tn * 4 + (8 << 20)),
        name="ada_mod",
    )(cvec, w_ada, b_ada.reshape(DEPTH, 1, n))


def _modnorm_kernel(x_ref, g_ref, shift_ref, scale_ref, o_ref):
    x = x_ref[...]
    y = x * lax.rsqrt(jnp.mean(x * x, axis=-1, keepdims=True) + EPS) * g_ref[...]
    o_ref[...] = (y * (1.0 + scale_ref[...]) + shift_ref[...]).astype(o_ref.dtype)


def _modnorm(x, gains, mod, layer, shift_col, scale_col):
    tm = 512
    return pl.pallas_call(
        _modnorm_kernel,
        grid=(N_TOK // tm,),
        in_specs=[
            pl.BlockSpec((tm, D_MODEL), lambda m: (m, 0)),
            pl.BlockSpec((None, 1, D_MODEL), lambda m: (layer, 0, 0)),
            pl.BlockSpec((None, None, 1, D_MODEL), lambda m: (layer, _mod_row(m * tm), 0, shift_col)),
            pl.BlockSpec((None, None, 1, D_MODEL), lambda m: (layer, _mod_row(m * tm), 0, scale_col)),
        ],
        out_specs=pl.BlockSpec((tm, D_MODEL), lambda m: (m, 0)),
        out_shape=jax.ShapeDtypeStruct((N_TOK, D_MODEL), BF16),
        compiler_params=_params(("arbitrary",), 6 * tm * D_MODEL * 4 + (8 << 20)),
        name="modnorm",
    )(x, gains.reshape(DEPTH, 1, D_MODEL), mod, mod)


IN_TILE = 1024
ZA_COLS = 6 * IN_TILE
HALF_TILE = IN_TILE // 2
GATE_SHIFT = W_GATES - W_RANK


def _in_proj_kernel(a_ref, w_ref, o_ref, wbf_ref):
    @pl.when(pl.program_id(1) == 0)
    def _():
        wbf_ref[...] = w_ref[...].astype(BF16)

    o_ref[...] = _dot_nt(a_ref[...], wbf_ref[...])


def _in_proj(a, w_t, layer):
    m, k = a.shape
    tm, tn = ROW_TILE, IN_TILE
    return pl.pallas_call(
        _in_proj_kernel,
        grid=(ZA_COLS // tn, m // tm),
        in_specs=[
            pl.BlockSpec((tm, k), lambda j, i: (i, 0)),
            pl.BlockSpec((None, tn, k), lambda j, i: (layer, j, 0)),
        ],
        out_specs=pl.BlockSpec((tm, tn), lambda j, i: (i, j)),
        out_shape=jax.ShapeDtypeStruct((m, ZA_COLS), F32),
        scratch_shapes=[pltpu.VMEM((tn, k), BF16)],
        compiler_params=_params(("arbitrary", "arbitrary"),
                                2 * (tm * k * 2 + k * tn * 4 + tm * tn * 4) + k * tn * 2 + (8 << 20)),
        name="in_proj",
    )(a, w_t)


def _gates_kernel(a_ref, wa_ref, wb_ref, wx_ref, o_ref, wbf_ref):
    @pl.when(pl.program_id(1) == 0)
    def _():
        head = HALF_TILE - GATE_SHIFT
        wbf_ref[0:head, :] = wa_ref[GATE_SHIFT:, :].astype(BF16)
        wbf_ref[head:head + HALF_TILE, :] = wb_ref[...].astype(BF16)
        wbf_ref[head + HALF_TILE:, :] = wx_ref[...].astype(BF16)

    o_ref[...] = _dot_nt(a_ref[...], wbf_ref[...]).astype(o_ref.dtype)


def _gates_proj(a, w_t, layer):
    m, k = a.shape
    tm, tn = ROW_TILE, IN_TILE
    n = N_BRANCH * D_MODEL
    blk0 = W_RANK // HALF_TILE
    assert blk0 * HALF_TILE == W_RANK and HALF_TILE % GATE_SHIFT == 0
    xper = HALF_TILE // GATE_SHIFT
    return pl.pallas_call(
        _gates_kernel,
        grid=(n // tn, m // tm),
        in_specs=[
            pl.BlockSpec((tm, k), lambda j, i: (i, 0)),
            pl.BlockSpec((None, HALF_TILE, k), lambda j, i: (layer, blk0 + 2 * j, 0)),
            pl.BlockSpec((None, HALF_TILE, k), lambda j, i: (layer, blk0 + 2 * j + 1, 0)),
            pl.BlockSpec((None, GATE_SHIFT, k), lambda j, i: (layer, (blk0 + 2 * j + 2) * xper, 0)),
        ],
        out_specs=pl.BlockSpec((tm, tn), lambda j, i: (i, j)),
        out_shape=jax.ShapeDtypeStruct((m, n), BF16),
        scratch_shapes=[pltpu.VMEM((tn, k), BF16)],
        compiler_params=_params(("arbitrary", "arbitrary"),
                                2 * (tm * k * 2 + k * (tn + GATE_SHIFT) * 4 + tm * tn * 4) + k * tn * 2 + (8 << 20)),
        name="gates_proj",
    )(a, w_t, w_t, w_t)


def _resid_kernel(a_ref, w_ref, x_ref, gate_ref, o_ref, wbf_ref):
    @pl.when(pl.program_id(1) == 0)
    def _():
        wbf_ref[...] = w_ref[...].astype(BF16)

    o_ref[...] = x_ref[...] + gate_ref[...] * _dot(a_ref[...], wbf_ref[...])


def _resid_proj(a, w, x, mod, layer, gate_col, tm, tn):
    m, k = a.shape
    n = D_MODEL
    gate_blk = gate_col * (D_MODEL // tn)
    return pl.pallas_call(
        _resid_kernel,
        grid=(n // tn, m // tm),
        in_specs=[
            pl.BlockSpec((tm, k), lambda j, i: (i, 0)),
            pl.BlockSpec((None, k, tn), lambda j, i: (layer, 0, j)),
            pl.BlockSpec((tm, tn), lambda j, i: (i, j)),
            pl.BlockSpec((None, None, 1, tn), lambda j, i: (layer, _mod_row(i * tm), 0, gate_blk + j)),
        ],
        out_specs=pl.BlockSpec((tm, tn), lambda j, i: (i, j)),
        out_shape=jax.ShapeDtypeStruct((m, n), F32),
        scratch_shapes=[pltpu.VMEM((k, tn), BF16)],
        compiler_params=_params(("arbitrary", "arbitrary"),
                                2 * (tm * k * 2 + k * tn * 4 + 2 * tm * tn * 4) + k * tn * 2 + (8 << 20)),
        name="resid_proj",
    )(a, w, x, mod)


def _dft_consts(t_len):
    kc = np.arange(F_GROUP_DIM)
    ang_c = 2.0 * np.pi * ((kc[:, None] * kc[None, :]) % F_GROUP_DIM) / F_GROUP_DIM
    chan = np.concatenate([np.cos(ang_c), np.sin(ang_c)], axis=1)
    kt = np.arange(t_len)
    ang_t = 2.0 * np.pi * ((kt[:, None] * kt[None, :]) % t_len) / t_len
    pos = np.concatenate([np.cos(ang_t), -np.sin(ang_t)], axis=1)
    return jnp.asarray(chan, F32), jnp.asarray(pos, F32)


def _fnet_kernel(x_ref, chan_ref, pos_ref, o_ref, u_ref, *, t_len):
    @pl.when(pl.program_id(1) == 0)
    def _():
        chan = chan_ref[...].astype(BF16)
        for g in range(F_GROUPS):
            cols = slice(g * F_GROUP_DIM, (g + 1) * F_GROUP_DIM)
            cs = _dot(x_ref[:, cols].astype(BF16), chan)
            u_ref[0:t_len, cols] = cs[:, :F_GROUP_DIM].astype(BF16)
            u_ref[t_len:2 * t_len, cols] = cs[:, F_GROUP_DIM:].astype(BF16)

    y = _dot(pos_ref[...].astype(BF16), u_ref[...]) * (1.0 / math.sqrt(t_len * F_GROUP_DIM))
    o_ref[...] = y.astype(o_ref.dtype)


def _fnet(za, row0, n_seq, t_len):
    chan, pos = _dft_consts(t_len)
    tq = 256
    nq = t_len // tq
    seq_blk0 = row0 // t_len
    return pl.pallas_call(
        functools.partial(_fnet_kernel, t_len=t_len),
        grid=(n_seq, nq),
        in_specs=[
            pl.BlockSpec((t_len, F_DIM), lambda b, i: (seq_blk0 + b, W_FIN // F_DIM)),
            pl.BlockSpec((F_GROUP_DIM, 2 * F_GROUP_DIM), lambda b, i: (0, 0)),
            pl.BlockSpec((tq, 2 * t_len), lambda b, i: (i, 0)),
        ],
        out_specs=pl.BlockSpec((tq, F_DIM), lambda b, i: (b * nq + i, 0)),
        out_shape=jax.ShapeDtypeStruct((n_seq * t_len, F_DIM), BF16),
        scratch_shapes=[pltpu.VMEM((2 * t_len, F_DIM), BF16)],
        compiler_params=_params(("arbitrary", "arbitrary"),
                                2 * (t_len * F_DIM * 4 + tq * 2 * t_len * 4 + tq * F_DIM * 2)
                                + 2 * t_len * F_DIM * 2 + tq * 2 * t_len * 2 + tq * F_DIM * 8 + (8 << 20)),
        name="fnet_%d" % t_len,
    )(za, chan, pos)


def _head_rms(x, g):
    return x * lax.rsqrt(jnp.mean(x * x, axis=-1, keepdims=True) + EPS) * g


def _rope_tables(t_len):
    rows = t_len // GRID_W
    row = np.repeat(np.arange(rows, dtype=np.float64), GRID_W)
    col = np.tile(np.arange(GRID_W, dtype=np.float64), rows)
    inv = ROPE_THETA ** (-np.arange(0, ROPE_AXIS_DIM, 2, dtype=np.float64) / ROPE_AXIS_DIM)
    ar = row[:, None] * inv
    ac = col[:, None] * inv
    cos = np.concatenate([np.cos(ar), np.cos(ar), np.cos(ac), np.cos(ac)], axis=1)
    sin = np.concatenate([-np.sin(ar), np.sin(ar), -np.sin(ac), np.sin(ac)], axis=1)
    return jnp.asarray(cos, F32), jnp.asarray(sin, F32)


def _rope(x, cos, sin):
    lane = lax.broadcasted_iota(jnp.int32, x.shape, 1)
    low = (lane % ROPE_AXIS_DIM) < (ROPE_AXIS_DIM // 2)
    partner = jnp.where(low, pltpu.roll(x, HEAD_DIM - ROPE_AXIS_DIM // 2, 1), pltpu.roll(x, ROPE_AXIS_DIM // 2, 1))
    return x * cos + partner * sin


Q_SCALE = HEAD_DIM ** -0.5 * math.log2(math.e)


def _with_ones(v):
    lane = lax.broadcasted_iota(jnp.int32, v.shape, 1)
    return jnp.concatenate([v, jnp.where(lane == 0, 1.0, 0.0).astype(v.dtype)], axis=1)


def _softmax_pv(q, kb, vb1):
    s = _dot_nt((q * Q_SCALE).astype(BF16), kb)
    p = jnp.exp2(s - jnp.max(s, axis=-1, keepdims=True)).astype(BF16)
    ov = _dot(p, vb1)
    return ov[:, :HEAD_DIM] / ov[:, HEAD_DIM:HEAD_DIM + 1]


def _attn_ctx_kernel(q_ref, k_ref, v_ref, qn_ref, kn_ref, o_ref, ko_ref, vo_ref):
    k = _head_rms(k_ref[...], kn_ref[...])
    v = v_ref[...]
    ko_ref[...] = k
    vo_ref[...] = v
    kb = k.astype(BF16)
    vb = _with_ones(v.astype(BF16))
    for g in range(N_GROUP):
        cols = slice(g * HEAD_DIM, (g + 1) * HEAD_DIM)
        q = _head_rms(q_ref[:, cols], qn_ref[...])
        o_ref[:, cols] = _softmax_pv(q, kb, vb).astype(o_ref.dtype)


def _attn_ctx(za, q_norm, k_norm, layer):
    gw = N_GROUP * HEAD_DIM
    return pl.pallas_call(
        _attn_ctx_kernel,
        grid=(BATCH, N_KV_HEADS),
        in_specs=[
            pl.BlockSpec((SEQ, gw), lambda b, h: (b, W_Q // gw + h)),
            pl.BlockSpec((SEQ, HEAD_DIM), lambda b, h: (b, W_K // HEAD_DIM + h)),
            pl.BlockSpec((SEQ, HEAD_DIM), lambda b, h: (b, W_V // HEAD_DIM + h)),
            pl.BlockSpec((None, 1, HEAD_DIM), lambda b, h: (layer, 0, 0)),
            pl.BlockSpec((None, 1, HEAD_DIM), lambda b, h: (layer, 0, 0)),
        ],
        out_specs=[
            pl.BlockSpec((SEQ, gw), lambda b, h: (b, h)),
            pl.BlockSpec((None, SEQ, HEAD_DIM), lambda b, h: (b, 0, h)),
            pl.BlockSpec((None, SEQ, HEAD_DIM), lambda b, h: (b, 0, h)),
        ],
        out_shape=[
            jax.ShapeDtypeStruct((N_CTX, ATTN_DIM), BF16),
            jax.ShapeDtypeStruct((BATCH, SEQ, KV_DIM), F32),
            jax.ShapeDtypeStruct((BATCH, SEQ, KV_DIM), F32),
        ],
        compiler_params=_params(("arbitrary", "arbitrary"), 16 << 20),
        name="attn_ctx",
    )(za, za, za, q_norm.reshape(DEPTH, 1, HEAD_DIM), k_norm.reshape(DEPTH, 1, HEAD_DIM))


def _attn_lat_kernel(q_ref, k_ref, v_ref, ck_ref, cv_ref, qn_ref, kn_ref, cosq_ref, sinq_ref,
                     cosk_ref, sink_ref, o_ref, kb_ref, vb_ref):
    @pl.when(pl.program_id(2) == 0)
    def _():
        kb_ref[0:PAST_LEN, :] = ck_ref[...].astype(BF16)
        vb_ref[0:PAST_LEN, :] = _with_ones(cv_ref[...].astype(BF16))
        k = _rope(_head_rms(k_ref[...], kn_ref[...]), cosk_ref[...], sink_ref[...])
        kb_ref[PAST_LEN:, :] = k.astype(BF16)
        vb_ref[PAST_LEN:, :] = _with_ones(v_ref[...].astype(BF16))

    kb = kb_ref[...]
    vb = vb_ref[...]
    for g in range(N_GROUP):
        cols = slice(g * HEAD_DIM, (g + 1) * HEAD_DIM)
        q = _rope(_head_rms(q_ref[:, cols], qn_ref[...]), cosq_ref[...], sinq_ref[...])
        o_ref[:, cols] = _softmax_pv(q, kb, vb).astype(o_ref.dtype)


def _attn_lat(za, cache_k, cache_v, q_norm, k_norm, layer):
    gw = N_GROUP * HEAD_DIM
    tq = 256
    nq = DEC_SEQ // tq
    cos, sin = _rope_tables(DEC_SEQ)
    ck = cache_k.reshape(DEC_BATCH, DEPTH, PAST_LEN, KV_DIM)
    cv = cache_v.reshape(DEC_BATCH, DEPTH, PAST_LEN, KV_DIM)
    seq0 = N_CTX // DEC_SEQ
    tile0 = N_CTX // tq
    return pl.pallas_call(
        _attn_lat_kernel,
        grid=(DEC_BATCH, N_KV_HEADS, nq),
        in_specs=[
            pl.BlockSpec((tq, gw), lambda b, h, i: (tile0 + b * nq + i, W_Q // gw + h)),
            pl.BlockSpec((DEC_SEQ, HEAD_DIM), lambda b, h, i: (seq0 + b, W_K // HEAD_DIM + h)),
            pl.BlockSpec((DEC_SEQ, HEAD_DIM), lambda b, h, i: (seq0 + b, W_V // HEAD_DIM + h)),
            pl.BlockSpec((None, None, PAST_LEN, HEAD_DIM), lambda b, h, i: (b, layer, 0, h)),
            pl.BlockSpec((None, None, PAST_LEN, HEAD_DIM), lambda b, h, i: (b, layer, 0, h)),
            pl.BlockSpec((None, 1, HEAD_DIM), lambda b, h, i: (layer, 0, 0)),
            pl.BlockSpec((None, 1, HEAD_DIM), lambda b, h, i: (layer, 0, 0)),
            pl.BlockSpec((tq, HEAD_DIM), lambda b, h, i: (i, 0)),
            pl.BlockSpec((tq, HEAD_DIM), lambda b, h, i: (i, 0)),
            pl.BlockSpec((DEC_SEQ, HEAD_DIM), lambda b, h, i: (0, 0)),
            pl.BlockSpec((DEC_SEQ, HEAD_DIM), lambda b, h, i: (0, 0)),
        ],
        out_specs=pl.BlockSpec((tq, gw), lambda b, h, i: (b * nq + i, h)),
        out_shape=jax.ShapeDtypeStruct((N_LAT, ATTN_DIM), BF16),
        scratch_shapes=[pltpu.VMEM((PAST_LEN + DEC_SEQ, HEAD_DIM), BF16),
                        pltpu.VMEM((PAST_LEN + DEC_SEQ, 2 * HEAD_DIM), BF16)],
        compiler_params=_params(("arbitrary", "arbitrary", "arbitrary"), 32 << 20),
        name="attn_lat",
    )(za, za, za, ck, cv, q_norm.reshape(DEPTH, 1, HEAD_DIM), k_norm.reshape(DEPTH, 1, HEAD_DIM),
      cos, sin, cos, sin)


def _gla_consts():
    c = GLA_CHUNK
    t = np.arange(c)
    cum_f = (t[None, :] <= t[:, None]).astype(np.float32)
    cum_b = (t[None, :] >= t[:, None]).astype(np.float32)
    upper, same = [], []
    for level in range(GLA_LEVELS):
        n = c >> level
        blk = t // n
        p = blk * n + n // 2
        upper.append(np.broadcast_to((t >= p).astype(np.float32)[:, None], (c, GLA_DK)))
        same.append((blk[:, None] == blk[None, :]).astype(np.float32))
    same.append(2.0 * np.eye(c, dtype=np.float32))
    return (jnp.asarray(cum_f, BF16), jnp.asarray(cum_b, BF16),
            jnp.asarray(np.stack(upper + [1.0 - u for u in upper]), F32), jnp.asarray(np.stack(same), F32))


def _chunk_select(t_len):
    nc = t_len // GLA_CHUNK
    sel = (np.arange(t_len)[:, None] // GLA_CHUNK == np.arange(LANES)[None, :]).astype(np.float32)
    assert nc <= LANES
    return jnp.asarray(sel, BF16)


def _split_hi_lo(x):
    hi = x.astype(BF16)
    lo = (x - hi.astype(F32)).astype(BF16)
    return jnp.concatenate([hi, lo], axis=1)


def _pivot_rows(b, level):
    c = GLA_CHUNK
    n = c >> level
    if n >= 16:
        parts = [jnp.broadcast_to(b[s + n // 2:s + n // 2 + 1, :], (n, GLA_DK)) for s in range(0, c, n)]
        return parts[0] if len(parts) == 1 else jnp.concatenate(parts, axis=0)
    b3 = b.reshape(c // 8, 8, GLA_DK)
    if n == 8:
        return jnp.broadcast_to(b3[:, 4:5, :], b3.shape).reshape(c, GLA_DK)
    if n == 4:
        sub = lax.broadcasted_iota(jnp.int32, b3.shape, 1)
        lo = jnp.broadcast_to(b3[:, 2:3, :], b3.shape)
        hi = jnp.broadcast_to(b3[:, 6:7, :], b3.shape)
        return jnp.where(sub < 4, lo, hi).reshape(c, GLA_DK)
    row = lax.broadcasted_iota(jnp.int32, b.shape, 0)
    return jnp.where((row & 1) == 0, pltpu.roll(b, c - 1, 0), b)


def _gla_decay_stage(h, ci, q_ref, k_ref, bf_ref, bb_ref, up_ref, qc_ref, slot):
    qt_ref, kt_ref, kd_ref = slot
    c = GLA_CHUNK
    rows = slice(ci * c, (ci + 1) * c)
    hk = slice(h * GLA_DK, (h + 1) * GLA_DK)
    q = q_ref[rows, hk] * (GLA_DK ** -0.5)
    k = k_ref[rows, hk]
    b_f = bf_ref[rows, hk]
    b_b = bb_ref[rows, hk]
    for level in range(GLA_LEVELS):
        wf = jnp.exp(-jnp.abs(b_f - _pivot_rows(b_f, level)))
        wb = jnp.exp(-jnp.abs(b_b - _pivot_rows(b_b, level)))
        up = up_ref[level]
        dn = up_ref[GLA_LEVELS + level]
        lv = slice(c * level, c * (level + 1))
        qt_ref[lv, :] = jnp.concatenate([q * (wf * up), q * (wb * dn)], axis=1).astype(BF16)
        kt_ref[lv, :] = jnp.concatenate([k * (wf * dn), k * (wb * up)], axis=1).astype(BF16)
    qc_ref[rows, h * GLA_DV:(h + 1) * GLA_DV] = jnp.concatenate(
        [q * jnp.exp(b_f), q * jnp.exp(b_b)], axis=1).astype(BF16)
    kd_ref[...] = jnp.concatenate([k * jnp.exp(b_f[c - 1:c, :] - b_f),
                                   k * jnp.exp(b_b[0:1, :] - b_b)], axis=1).astype(BF16)


def _head_cols(refs, h):
    width = refs[0].shape[1]
    start = h * GLA_DV
    return refs[start // width], slice(start % width, start % width + GLA_DV)


def _gla_matmul_stage(h, ci, nc, q_ref, k_ref, v_refs, same_ref, oi_ref, kv_ref, slot):
    qt_ref, kt_ref, kd_ref = slot
    c = GLA_CHUNK
    rows = slice(ci * c, (ci + 1) * c)
    hk = slice(h * GLA_DK, (h + 1) * GLA_DK)
    q = (q_ref[rows, hk] * (GLA_DK ** -0.5)).astype(BF16)
    k = k_ref[rows, hk].astype(BF16)
    v_ref, vcols = _head_cols(v_refs, h)
    v = v_ref[rows, vcols].astype(BF16)
    att = _dot_nt(q, k) * same_ref[GLA_LEVELS]
    for level in range(GLA_LEVELS):
        lv = slice(c * level, c * (level + 1))
        att = att + _dot_nt(qt_ref[lv, :], kt_ref[lv, :]) * same_ref[level]
    oi_ref[rows, h * GLA_DV:(h + 1) * GLA_DV] = _dot(att.astype(BF16), v)
    kv_ref[h * nc + ci] = _dot_tn(kd_ref[...], v)


def _gla_kernel(*refs, t_len, hps, nv, has_state):
    refs = list(refs)
    q_ref, k_ref = refs[0:2]
    v_refs = refs[2:2 + nv]
    r_refs = refs[2 + nv:2 + 2 * nv]
    (zr_ref, wf_ref, wb_ref, bf_in_ref, bb_in_ref, gn_ref,
     cumf_ref, cumb_ref, up_ref, same_ref, sel_ref) = refs[2 + 2 * nv:13 + 2 * nv]
    rest = refs[13 + 2 * nv:]
    if has_state:
        s0f_ref, s0b_ref = rest[0:2]
        rest = rest[2:]
    (o_ref, sf_ref, sb_ref, bf_ref, bb_ref, oi_ref, qc_ref, kv_ref, st_ref,
     qt0, kt0, kd0, qt1, kt1, kd1) = rest
    c = GLA_CHUNK
    nc = t_len // c
    width = hps * GLA_DK
    slots = ((qt0, kt0, kd0), (qt1, kt1, kd1))
    zr = zr_ref[...].astype(BF16)
    laf = _log_sigmoid(_dot(zr, wf_ref[...]) + bf_in_ref[...]) * (1.0 / GLA_GATE_TAU)
    lab = _log_sigmoid(_dot(zr, wb_ref[...]) + bb_in_ref[...]) * (1.0 / GLA_GATE_TAU)
    bf_ref[...] = laf
    bb_ref[...] = lab

    def chunk_totals(la):
        s = _dot_tn(_split_hi_lo(la), sel_ref[...])
        return jnp.exp(s[:width, :] + s[width:, :])

    tot_f = chunk_totals(laf)
    tot_b = chunk_totals(lab)

    for ci in range(nc):
        rows = slice(ci * c, (ci + 1) * c)
        sf = _dot(cumf_ref[...], _split_hi_lo(bf_ref[rows, :]))
        sb = _dot(cumb_ref[...], _split_hi_lo(bb_ref[rows, :]))
        bf_ref[rows, :] = sf[:, :width] + sf[:, width:]
        bb_ref[rows, :] = sb[:, :width] + sb[:, width:]

    decay = functools.partial(_gla_decay_stage, q_ref=q_ref, k_ref=k_ref, bf_ref=bf_ref, bb_ref=bb_ref,
                              up_ref=up_ref, qc_ref=qc_ref)
    matmuls = functools.partial(_gla_matmul_stage, nc=nc, q_ref=q_ref, k_ref=k_ref, v_refs=v_refs,
                                same_ref=same_ref, oi_ref=oi_ref, kv_ref=kv_ref)
    items = [(h, ci) for h in range(hps) for ci in range(nc)]
    decay(*items[0], slot=slots[0])
    for n, item in enumerate(items):
        if n + 1 < len(items):
            decay(*items[n + 1], slot=slots[(n + 1) % 2])
        matmuls(*item, slot=slots[n % 2])

    for h in range(hps):
        hk = slice(h * GLA_DK, (h + 1) * GLA_DK)
        s = s0f_ref[...] if has_state else jnp.zeros((GLA_DK, GLA_DV), F32)
        for ci in range(nc):
            st_ref[h * nc + ci, 0:GLA_DK, :] = s.astype(BF16)
            s = tot_f[hk, ci:ci + 1] * s + kv_ref[h * nc + ci, 0:GLA_DK, :]
        sf_ref[h] = s
        s = s0b_ref[...] if has_state else jnp.zeros((GLA_DK, GLA_DV), F32)
        for ci in reversed(range(nc)):
            st_ref[h * nc + ci, GLA_DK:2 * GLA_DK, :] = s.astype(BF16)
            s = tot_b[hk, ci:ci + 1] * s + kv_ref[h * nc + ci, GLA_DK:2 * GLA_DK, :]
        sb_ref[h] = s

    for h, ci in items:
        rows = slice(ci * c, (ci + 1) * c)
        hv = slice(h * GLA_DV, (h + 1) * GLA_DV)
        o = oi_ref[rows, hv] + _dot(qc_ref[rows, hv], st_ref[h * nc + ci])
        o = o * lax.rsqrt(jnp.mean(o * o, axis=-1, keepdims=True) + EPS) * gn_ref[...]
        r_ref, rcols = _head_cols(r_refs, h)
        r = r_ref[rows, rcols]
        o_ref[rows, hv] = (o * (r * _sigmoid(r))).astype(o_ref.dtype)


def _gla(za, wgf, wgb, b_gate_f, b_gate_b, gla_norm, layer, row0, n_seq, t_len, hps, s0f=None, s0b=None):
    has_state = s0f is not None
    assert not has_state or hps == 1
    cum_f, cum_b, upper, same = _gla_consts()
    sel = _chunk_select(t_len)
    seq0 = row0 // t_len
    nc = t_len // GLA_CHUNK
    width = hps * GLA_DK
    vw = min(hps, 2) * GLA_DV
    nv = hps * GLA_DV // vw
    const = lambda shape: pl.BlockSpec(shape, lambda b, h: (0,) * len(shape))
    wide = lambda col0, j: pl.BlockSpec((t_len, vw), lambda b, h: (seq0 + b, col0 // vw + h * nv + j))
    in_specs = (
        [pl.BlockSpec((t_len, width), lambda b, h: (seq0 + b, W_GQ // width + h)),
         pl.BlockSpec((t_len, width), lambda b, h: (seq0 + b, W_GK // width + h))]
        + [wide(W_GV, j) for j in range(nv)] + [wide(W_GR, j) for j in range(nv)]
        + [pl.BlockSpec((t_len, LANES), lambda b, h: (seq0 + b, W_RANK // LANES)),
           pl.BlockSpec((None, LANES, width), lambda b, h: (layer, 0, h)),
           pl.BlockSpec((None, LANES, width), lambda b, h: (layer, 0, h)),
           pl.BlockSpec((None, 1, width), lambda b, h: (layer, 0, h)),
           pl.BlockSpec((None, 1, width), lambda b, h: (layer, 0, h)),
           pl.BlockSpec((None, 1, GLA_DV), lambda b, h: (layer, 0, 0)),
           const(cum_f.shape), const(cum_b.shape), const(upper.shape), const(same.shape), const(sel.shape)])
    args = [za] * (3 + 2 * nv) + [wgf, wgb, b_gate_f.reshape(DEPTH, 1, GLA_K_DIM),
                                  b_gate_b.reshape(DEPTH, 1, GLA_K_DIM), gla_norm.reshape(DEPTH, 1, GLA_DV),
                                  cum_f, cum_b, upper, same, sel]
    if has_state:
        in_specs += [
            pl.BlockSpec((None, None, None, GLA_DK, GLA_DV), lambda b, h: (b, layer, h, 0, 0)),
            pl.BlockSpec((None, None, None, GLA_DK, GLA_DV), lambda b, h: (b, layer, h, 0, 0)),
        ]
        args += [s0f, s0b]
    return pl.pallas_call(
        functools.partial(_gla_kernel, t_len=t_len, hps=hps, nv=nv, has_state=has_state),
        grid=(n_seq, GLA_HEADS // hps),
        in_specs=in_specs,
        out_specs=[
            pl.BlockSpec((t_len, hps * GLA_DV), lambda b, h: (b, h)),
            pl.BlockSpec((None, hps, GLA_DK, GLA_DV), lambda b, h: (b, h, 0, 0)),
            pl.BlockSpec((None, hps, GLA_DK, GLA_DV), lambda b, h: (b, h, 0, 0)),
        ],
        out_shape=[
            jax.ShapeDtypeStruct((n_seq * t_len, GLA_V_DIM), BF16),
            jax.ShapeDtypeStruct((n_seq, GLA_HEADS, GLA_DK, GLA_DV), F32),
            jax.ShapeDtypeStruct((n_seq, GLA_HEADS, GLA_DK, GLA_DV), F32),
        ],
        scratch_shapes=[
            pltpu.VMEM((t_len, width), F32), pltpu.VMEM((t_len, width), F32),
            pltpu.VMEM((t_len, hps * GLA_DV), F32), pltpu.VMEM((t_len, hps * GLA_DV), BF16),
            pltpu.VMEM((hps * nc, 2 * GLA_DK, GLA_DV), F32), pltpu.VMEM((hps * nc, 2 * GLA_DK, GLA_DV), BF16),
        ] + 2 * [pltpu.VMEM((GLA_CHUNK * GLA_LEVELS, 2 * GLA_DK), BF16),
                 pltpu.VMEM((GLA_CHUNK * GLA_LEVELS, 2 * GLA_DK), BF16),
                 pltpu.VMEM((GLA_CHUNK, 2 * GLA_DK), BF16)],
        compiler_params=_params(("arbitrary", "arbitrary"), 40 << 20),
        name="gla_%d" % t_len,
    )(*args)


def _merge_kernel(fc_ref, fl_ref, ac_ref, al_ref, gc_ref, gl_ref, wa_ref, wb_ref, wc_ref,
                  ga_ref, gb_ref, gg_ref, o_ref, wa_s, wb_s, wc_s, *, ctx_tiles):
    i = pl.program_id(1)

    @pl.when(i == 0)
    def _():
        wa_s[...] = wa_ref[...].astype(BF16)
        wb_s[...] = wb_ref[...].astype(BF16)
        wc_s[...] = wc_ref[...].astype(BF16)

    def compute(f_ref, a_ref, g_ref):
        acc = _sigmoid(ga_ref[...].astype(F32)) * _dot(f_ref[...], wa_s[...])
        acc = acc + _sigmoid(gb_ref[...].astype(F32)) * _dot(a_ref[...], wb_s[...])
        acc = acc + _sigmoid(gg_ref[...].astype(F32)) * _dot(g_ref[...], wc_s[...])
        o_ref[...] = acc.astype(o_ref.dtype)

    @pl.when(i < ctx_tiles)
    def _():
        compute(fc_ref, ac_ref, gc_ref)

    @pl.when(i >= ctx_tiles)
    def _():
        compute(fl_ref, al_ref, gl_ref)


def _merge(fa, at, gl, w_fourier, w_attn, w_gla, zg, layer):
    tm, tn = 256, 1024
    k = F_DIM
    nj = D_MODEL // tn
    ctx_tiles = N_CTX // tm
    ctx = pl.BlockSpec((tm, k), lambda j, i: (jnp.minimum(i, ctx_tiles - 1), 0))
    lat = pl.BlockSpec((tm, k), lambda j, i: (jnp.maximum(i - ctx_tiles, 0), 0))
    wsp = pl.BlockSpec((None, k, tn), lambda j, i: (layer, 0, j))
    gate = lambda br: pl.BlockSpec((tm, tn), lambda j, i: (i, br * nj + j))
    return pl.pallas_call(
        functools.partial(_merge_kernel, ctx_tiles=ctx_tiles),
        grid=(nj, N_TOK // tm),
        in_specs=[ctx, lat, ctx, lat, ctx, lat, wsp, wsp, wsp, gate(0), gate(1), gate(2)],
        out_specs=pl.BlockSpec((tm, tn), lambda j, i: (i, j)),
        out_shape=jax.ShapeDtypeStruct((N_TOK, D_MODEL), BF16),
        scratch_shapes=[pltpu.VMEM((k, tn), BF16)] * 3,
        compiler_params=_params(("arbitrary", "arbitrary"),
                                2 * (6 * tm * k * 2 + 3 * k * tn * 4 + 3 * tm * tn * 4 + tm * tn * 2)
                                + 3 * k * tn * 2 + 4 * tm * tn * 4 + (8 << 20)),
        name="merge",
    )(fa[0], fa[1], at[0], at[1], gl[0], gl[1], w_fourier, w_attn, w_gla, zg, zg, zg)


def _convffn_up_kernel(h_ref, wg_ref, wv_ref, cwg_ref, cwv_ref, cbg_ref, cbv_ref, o_ref, wg_s, wv_s, *, tm):
    @pl.when(pl.program_id(1) == 0)
    def _():
        wg_s[...] = wg_ref[...].astype(BF16)
        wv_s[...] = wv_ref[...].astype(BF16)

    row0 = pl.program_id(1) * tm
    seq_len = jnp.where(row0 < N_CTX, SEQ, DEC_SEQ)
    h = h_ref[...]
    pos = lax.broadcasted_iota(jnp.int32, (tm, 1), 0) & (seq_len - 1)
    has_prev = (pos != 0).astype(F32)
    has_next = (pos != seq_len - 1).astype(F32)

    def conv(u, cw_ref, cb_ref):
        prev = pltpu.roll(u, 1, 0) * has_prev
        nxt = pltpu.roll(u, tm - 1, 0) * has_next
        return prev * cw_ref[0:1, :] + u * cw_ref[1:2, :] + nxt * cw_ref[2:3, :] + cb_ref[...]

    g = conv(_dot(h, wg_s[...]), cwg_ref, cbg_ref)
    val = conv(_dot(h, wv_s[...]), cwv_ref, cbv_ref)
    o_ref[...] = (g * _sigmoid(g) * val).astype(o_ref.dtype)


def _convffn_up(h, w_up, conv_w, conv_b, layer):
    tm, tn = ROW_TILE, 512
    k = D_MODEL
    nj = D_FF // tn
    cb = conv_b.reshape(DEPTH, 1, 2 * D_FF)
    return pl.pallas_call(
        functools.partial(_convffn_up_kernel, tm=tm),
        grid=(nj, N_TOK // tm),
        in_specs=[
            pl.BlockSpec((tm, k), lambda j, i: (i, 0)),
            pl.BlockSpec((None, k, tn), lambda j, i: (layer, 0, j)),
            pl.BlockSpec((None, k, tn), lambda j, i: (layer, 0, nj + j)),
            pl.BlockSpec((None, 3, tn), lambda j, i: (layer, 0, j)),
            pl.BlockSpec((None, 3, tn), lambda j, i: (layer, 0, nj + j)),
            pl.BlockSpec((None, 1, tn), lambda j, i: (layer, 0, j)),
            pl.BlockSpec((None, 1, tn), lambda j, i: (layer, 0, nj + j)),
        ],
        out_specs=pl.BlockSpec((tm, tn), lambda j, i: (i, j)),
        out_shape=jax.ShapeDtypeStruct((N_TOK, D_FF), BF16),
        scratch_shapes=[pltpu.VMEM((k, tn), BF16)] * 2,
        compiler_params=_params(("arbitrary", "arbitrary"),
                                2 * (tm * k * 2 + 2 * k * tn * 4 + tm * tn * 2) + 2 * k * tn * 2
                                + 8 * tm * tn * 4 + (8 << 20)),
        name="convffn_up",
    )(h, w_up, w_up, conv_w, conv_w, cb, cb)


def _final_norm_kernel(x_ref, g_ref, oc_ref, ol_ref, *, ctx_tiles):
    i = pl.program_id(0)
    x = x_ref[...]
    y = x * lax.rsqrt(jnp.mean(x * x, axis=-1, keepdims=True) + EPS) * g_ref[...]

    @pl.when(i < ctx_tiles)
    def _():
        oc_ref[...] = y

    @pl.when(i >= ctx_tiles)
    def _():
        ol_ref[...] = y


def _final_norm(x, g):
    tm = 512
    ctx_tiles = N_CTX // tm
    return pl.pallas_call(
        functools.partial(_final_norm_kernel, ctx_tiles=ctx_tiles),
        grid=(N_TOK // tm,),
        in_specs=[pl.BlockSpec((tm, D_MODEL), lambda m: (m, 0)),
                  pl.BlockSpec((1, D_MODEL), lambda m: (0, 0))],
        out_specs=[pl.BlockSpec((tm, D_MODEL), lambda m: (jnp.minimum(m, ctx_tiles - 1), 0)),
                   pl.BlockSpec((tm, D_MODEL), lambda m: (jnp.maximum(m - ctx_tiles, 0), 0))],
        out_shape=[jax.ShapeDtypeStruct((N_CTX, D_MODEL), F32), jax.ShapeDtypeStruct((N_LAT, D_MODEL), F32)],
        compiler_params=_params(("arbitrary",), 8 * tm * D_MODEL * 4 + (8 << 20)),
        name="final_norm",
    )(x, g.reshape(1, D_MODEL))


def _pad_gate_w(w_gate, row0):
    out = jnp.zeros((DEPTH, LANES, GLA_K_DIM), BF16)
    return out.at[:, row0:row0 + GLA_GATE_RANK, :].set(w_gate.astype(BF16))


def kernel(x_prompt, x_sample, cache_k, cache_v, state_gla_fwd, state_gla_bwd, c, c_ctx, w_ada, b_ada, norm1, w_in, q_norm, k_norm, w_fourier, w_attn, w_gate_f, b_gate_f, w_gate_b, b_gate_b, gla_norm, w_gla, w_out, norm2, w_up, conv_w, conv_b, w_down, final_norm):
    x = jnp.concatenate([x_prompt.reshape(N_CTX, D_MODEL), x_sample.reshape(N_LAT, D_MODEL)], axis=0)
    cvec = jnp.concatenate([c_ctx[None, :], c, jnp.zeros((8 - 1 - DEC_BATCH, D_MODEL), F32)], axis=0)
    mod = _ada(cvec, w_ada, b_ada).reshape(DEPTH, 8, 1, N_MOD * D_MODEL)
    w_t = jnp.swapaxes(w_in, 1, 2)
    wgf = _pad_gate_w(w_gate_f, 0)
    wgb = _pad_gate_w(w_gate_b, GLA_GATE_RANK)

    new_k, new_v, new_sf, new_sb = [], [], [], []
    for l in range(DEPTH):
        h = _modnorm(x, norm1, mod, l, 0, 1)
        za = _in_proj(h, w_t, l)
        zg = _gates_proj(h, w_t, l)
        fa = (_fnet(za, 0, BATCH, SEQ), _fnet(za, N_CTX, DEC_BATCH, DEC_SEQ))
        at_ctx, k_ctx, v_ctx = _attn_ctx(za, q_norm, k_norm, l)
        at = (at_ctx, _attn_lat(za, cache_k, cache_v, q_norm, k_norm, l))
        gl_ctx, sf, sb = _gla(za, wgf, wgb, b_gate_f, b_gate_b, gla_norm, l, 0, BATCH, SEQ, GLA_HEADS)
        gl_lat, _, _ = _gla(za, wgf, wgb, b_gate_f, b_gate_b, gla_norm, l, N_CTX, DEC_BATCH, DEC_SEQ, 1,
                            state_gla_fwd, state_gla_bwd)
        merged = _merge(fa, at, (gl_ctx, gl_lat), w_fourier, w_attn, w_gla, zg, l)
        x = _resid_proj(merged, w_out, x, mod, l, 2, ROW_TILE, 1024)
        h = _modnorm(x, norm2, mod, l, 3, 4)
        hmid = _convffn_up(h, w_up, conv_w, conv_b, l)
        x = _resid_proj(hmid, w_down, x, mod, l, 5, 512, 512)
        new_k.append(k_ctx)
        new_v.append(v_ctx)
        new_sf.append(sf)
        new_sb.append(sb)

    y_ctx, y_lat = _final_norm(x, final_norm)
    y_prompt = y_ctx.reshape(BATCH, SEQ, D_MODEL)
    y_sample = y_lat.reshape(DEC_BATCH, DEC_SEQ, D_MODEL)
    kv_shape = (BATCH, DEPTH, SEQ, N_KV_HEADS, HEAD_DIM)
    return (y_prompt, y_sample,
            jnp.stack(new_k, axis=1).reshape(kv_shape), jnp.stack(new_v, axis=1).reshape(kv_shape),
            jnp.stack(new_sf, axis=1), jnp.stack(new_sb, axis=1))
```

```python
import functools
import math

import numpy as np
import jax
import jax.numpy as jnp
from jax import lax
from jax.experimental import pallas as pl
from jax.experimental.pallas import tpu as pltpu

F32 = jnp.float32
BF16 = jnp.bfloat16

D_MODEL = 2048
BATCH = 16
SEQ = 256
DEPTH = 4
DEC_BATCH = 4
DEC_SEQ = 1024
PAST_LEN = 256
GRID_W = 64
HEAD_DIM = 128
N_Q_HEADS = 8
N_KV_HEADS = 2
N_GROUP = N_Q_HEADS // N_KV_HEADS
ATTN_DIM = N_Q_HEADS * HEAD_DIM
KV_DIM = N_KV_HEADS * HEAD_DIM
ROPE_THETA = 10000.0
ROPE_AXIS_DIM = HEAD_DIM // 2
F_GROUPS = 8
F_GROUP_DIM = 128
F_DIM = F_GROUPS * F_GROUP_DIM
GLA_HEADS = 4
GLA_DK = 128
GLA_DV = 256
GLA_K_DIM = GLA_HEADS * GLA_DK
GLA_V_DIM = GLA_HEADS * GLA_DV
GLA_GATE_RANK = 16
GLA_GATE_TAU = 16.0
D_FF = 5632
N_MOD = 6
N_BRANCH = 3
EPS = 1e-6

N_CTX = BATCH * SEQ
N_LAT = DEC_BATCH * DEC_SEQ
N_TOK = N_CTX + N_LAT

VMEM_CAP_BYTES = 60 * 1024 * 1024
LANES = 128

W_FIN = 0
W_Q = W_FIN + F_DIM
W_K = W_Q + ATTN_DIM
W_V = W_K + KV_DIM
W_GQ = W_V + KV_DIM
W_GK = W_GQ + GLA_K_DIM
W_GV = W_GK + GLA_K_DIM
W_GR = W_GV + GLA_V_DIM
W_RANK = W_GR + GLA_V_DIM
W_GATES = W_RANK + 2 * GLA_GATE_RANK
N_IN = W_GATES + N_BRANCH * D_MODEL

GLA_CHUNK = 64
GLA_LEVELS = 6
ROW_TILE = 1024


def _params(semantics, vmem_bytes):
    return pltpu.CompilerParams(dimension_semantics=semantics,
                                vmem_limit_bytes=int(min(vmem_bytes, VMEM_CAP_BYTES)))


def _mod_row(row_start):
    return jnp.where(row_start < N_CTX, 0, 1 + (row_start - N_CTX) // DEC_SEQ)


def _sigmoid(x):
    return 1.0 / (1.0 + jnp.exp(-x))


def _log_sigmoid(x):
    return jnp.minimum(x, 0.0) - jnp.log(1.0 + jnp.exp(-jnp.abs(x)))


def _dot(a, b):
    return jnp.dot(a, b, preferred_element_type=F32)


def _dot_nt(a, b):
    return lax.dot_general(a, b, (((1,), (1,)), ((), ())), preferred_element_type=F32)


def _dot_tn(a, b):
    return lax.dot_general(a, b, (((0,), (0,)), ((), ())), preferred_element_type=F32)


def _ada_kernel(c_ref, w_ref, b_ref, o_ref):
    c = c_ref[...]
    a = (c * _sigmoid(c)).astype(BF16)
    o_ref[...] = _dot(a, w_ref[...].astype(BF16)) + b_ref[...]


def _ada(cvec, w_ada, b_ada):
    tn = 2048
    n = N_MOD * D_MODEL
    return pl.pallas_call(
        _ada_kernel,
        grid=(DEPTH, n // tn),
        in_specs=[
            pl.BlockSpec((8, D_MODEL), lambda l, j: (0, 0)),
            pl.BlockSpec((None, D_MODEL, tn), lambda l, j: (l, 0, j)),
            pl.BlockSpec((None, 1, tn), lambda l, j: (l, 0, j)),
        ],
        out_specs=pl.BlockSpec((None, 8, tn), lambda l, j: (l, 0, j)),
        out_shape=jax.ShapeDtypeStruct((DEPTH, 8, n), F32),
        compiler_params=_params(("arbitrary", "arbitrary"), 3 * D_MODEL * tn * 4 + (8 << 20)),
        name="ada_mod",
    )(cvec, w_ada, b_ada.reshape(DEPTH, 1, n))


def _modnorm_kernel(x_ref, g_ref, shift_ref, scale_ref, o_ref):
    x = x_ref[...]
    y = x * lax.rsqrt(jnp.mean(x * x, axis=-1, keepdims=True) + EPS) * g_ref[...]
    o_ref[...] = (y * (1.0 + scale_ref[...]) + shift_ref[...]).astype(o_ref.dtype)


def _modnorm(x, gains, mod, layer, shift_col, scale_col):
    tm = ROW_TILE
    return pl.pallas_call(
        _modnorm_kernel,
        grid=(N_TOK // tm,),
        in_specs=[
            pl.BlockSpec((tm, D_MODEL), lambda m: (m, 0)),
            pl.BlockSpec((None, 1, D_MODEL), lambda m: (layer, 0, 0)),
            pl.BlockSpec((None, None, 1, D_MODEL), lambda m: (layer, _mod_row(m * tm), 0, shift_col)),
            pl.BlockSpec((None, None, 1, D_MODEL), lambda m: (layer, _mod_row(m * tm), 0, scale_col)),
        ],
        out_specs=pl.BlockSpec((tm, D_MODEL), lambda m: (m, 0)),
        out_shape=jax.ShapeDtypeStruct((N_TOK, D_MODEL), BF16),
        compiler_params=_params(("arbitrary",), 6 * tm * D_MODEL * 4 + (8 << 20)),
        name="modnorm",
    )(x, gains.reshape(DEPTH, 1, D_MODEL), mod, mod)


IN_TILE = 1024
ZA_COLS = 6 * IN_TILE
HALF_TILE = IN_TILE // 2
GATE_SHIFT = W_GATES - W_RANK


def _in_proj_kernel(a_ref, w_ref, o_ref, wbf_ref):
    @pl.when(pl.program_id(1) == 0)
    def _():
        wbf_ref[...] = w_ref[...].astype(BF16)

    o_ref[...] = _dot_nt(a_ref[...], wbf_ref[...])


def _in_proj(a, w_t, layer):
    m, k = a.shape
    tm, tn = ROW_TILE, IN_TILE
    return pl.pallas_call(
        _in_proj_kernel,
        grid=(ZA_COLS // tn, m // tm),
        in_specs=[
            pl.BlockSpec((tm, k), lambda j, i: (i, 0)),
            pl.BlockSpec((None, tn, k), lambda j, i: (layer, j, 0)),
        ],
        out_specs=pl.BlockSpec((tm, tn), lambda j, i: (i, j)),
        out_shape=jax.ShapeDtypeStruct((m, ZA_COLS), F32),
        scratch_shapes=[pltpu.VMEM((tn, k), BF16)],
        compiler_params=_params(("arbitrary", "arbitrary"),
                                2 * (tm * k * 2 + k * tn * 4 + tm * tn * 4) + k * tn * 2 + (8 << 20)),
        name="in_proj",
    )(a, w_t)


def _gates_kernel(a_ref, wa_ref, wb_ref, wx_ref, o_ref, wbf_ref):
    @pl.when(pl.program_id(1) == 0)
    def _():
        head = HALF_TILE - GATE_SHIFT
        wbf_ref[0:head, :] = wa_ref[GATE_SHIFT:, :].astype(BF16)
        wbf_ref[head:head + HALF_TILE, :] = wb_ref[...].astype(BF16)
        wbf_ref[head + HALF_TILE:, :] = wx_ref[...].astype(BF16)

    o_ref[...] = _dot_nt(a_ref[...], wbf_ref[...]).astype(o_ref.dtype)


def _gates_proj(a, w_t, layer):
    m, k = a.shape
    tm, tn = ROW_TILE, IN_TILE
    n = N_BRANCH * D_MODEL
    blk0 = W_RANK // HALF_TILE
    assert blk0 * HALF_TILE == W_RANK and HALF_TILE % GATE_SHIFT == 0
    xper = HALF_TILE // GATE_SHIFT
    return pl.pallas_call(
        _gates_kernel,
        grid=(n // tn, m // tm),
        in_specs=[
            pl.BlockSpec((tm, k), lambda j, i: (i, 0)),
            pl.BlockSpec((None, HALF_TILE, k), lambda j, i: (layer, blk0 + 2 * j, 0)),
            pl.BlockSpec((None, HALF_TILE, k), lambda j, i: (layer, blk0 + 2 * j + 1, 0)),
            pl.BlockSpec((None, GATE_SHIFT, k), lambda j, i: (layer, (blk0 + 2 * j + 2) * xper, 0)),
        ],
        out_specs=pl.BlockSpec((tm, tn), lambda j, i: (i, j)),
        out_shape=jax.ShapeDtypeStruct((m, n), BF16),
        scratch_shapes=[pltpu.VMEM((tn, k), BF16)],
        compiler_params=_params(("arbitrary", "arbitrary"),
                                2 * (tm * k * 2 + k * (tn + GATE_SHIFT) * 4 + tm * tn * 4) + k * tn * 2 + (8 << 20)),
        name="gates_proj",
    )(a, w_t, w_t, w_t)


def _resid_kernel(a_ref, w_ref, x_ref, gate_ref, o_ref, wbf_ref):
    @pl.when(pl.program_id(1) == 0)
    def _():
        wbf_ref[...] = w_ref[...].astype(BF16)

    o_ref[...] = x_ref[...] + gate_ref[...] * _dot(a_ref[...], wbf_ref[...])


def _resid_proj(a, w, x, mod, layer, gate_col, tm, tn):
    m, k = a.shape
    n = D_MODEL
    gate_blk = gate_col * (D_MODEL // tn)
    return pl.pallas_call(
        _resid_kernel,
        grid=(n // tn, m // tm),
        in_specs=[
            pl.BlockSpec((tm, k), lambda j, i: (i, 0)),
            pl.BlockSpec((None, k, tn), lambda j, i: (layer, 0, j)),
            pl.BlockSpec((tm, tn), lambda j, i: (i, j)),
            pl.BlockSpec((None, None, 1, tn), lambda j, i: (layer, _mod_row(i * tm), 0, gate_blk + j)),
        ],
        out_specs=pl.BlockSpec((tm, tn), lambda j, i: (i, j)),
        out_shape=jax.ShapeDtypeStruct((m, n), F32),
        scratch_shapes=[pltpu.VMEM((k, tn), BF16)],
        compiler_params=_params(("arbitrary", "arbitrary"),
                                2 * (tm * k * 2 + k * tn * 4 + 2 * tm * tn * 4) + k * tn * 2 + (8 << 20)),
        name="resid_proj",
    )(a, w, x, mod)


def _dft_consts(t_len):
    kc = np.arange(F_GROUP_DIM)
    ang_c = 2.0 * np.pi * ((kc[:, None] * kc[None, :]) % F_GROUP_DIM) / F_GROUP_DIM
    chan = np.concatenate([np.cos(ang_c), np.sin(ang_c)], axis=1)
    kt = np.arange(t_len)
    ang_t = 2.0 * np.pi * ((kt[:, None] * kt[None, :]) % t_len) / t_len
    pos = np.concatenate([np.cos(ang_t), -np.sin(ang_t)], axis=1)
    return jnp.asarray(chan, F32), jnp.asarray(pos, F32)


def _fnet_kernel(x_ref, chan_ref, pos_ref, o_ref, u_ref, *, t_len):
    @pl.when(pl.program_id(1) == 0)
    def _():
        chan = chan_ref[...].astype(BF16)
        for g in range(F_GROUPS):
            cols = slice(g * F_GROUP_DIM, (g + 1) * F_GROUP_DIM)
            cs = _dot(x_ref[:, cols].astype(BF16), chan)
            u_ref[0:t_len, cols] = cs[:, :F_GROUP_DIM].astype(BF16)
            u_ref[t_len:2 * t_len, cols] = cs[:, F_GROUP_DIM:].astype(BF16)

    y = _dot(pos_ref[...].astype(BF16), u_ref[...]) * (1.0 / math.sqrt(t_len * F_GROUP_DIM))
    o_ref[...] = y.astype(o_ref.dtype)


def _fnet(za, row0, n_seq, t_len):
    chan, pos = _dft_consts(t_len)
    tq = 256
    nq = t_len // tq
    seq_blk0 = row0 // t_len
    return pl.pallas_call(
        functools.partial(_fnet_kernel, t_len=t_len),
        grid=(n_seq, nq),
        in_specs=[
            pl.BlockSpec((t_len, F_DIM), lambda b, i: (seq_blk0 + b, W_FIN // F_DIM)),
            pl.BlockSpec((F_GROUP_DIM, 2 * F_GROUP_DIM), lambda b, i: (0, 0)),
            pl.BlockSpec((tq, 2 * t_len), lambda b, i: (i, 0)),
        ],
        out_specs=pl.BlockSpec((tq, F_DIM), lambda b, i: (b * nq + i, 0)),
        out_shape=jax.ShapeDtypeStruct((n_seq * t_len, F_DIM), BF16),
        scratch_shapes=[pltpu.VMEM((2 * t_len, F_DIM), BF16)],
        compiler_params=_params(("arbitrary", "arbitrary"),
                                2 * (t_len * F_DIM * 4 + tq * 2 * t_len * 4 + tq * F_DIM * 2)
                                + 2 * t_len * F_DIM * 2 + tq * 2 * t_len * 2 + tq * F_DIM * 8 + (8 << 20)),
        name="fnet_%d" % t_len,
    )(za, chan, pos)


def _head_rms(x, g):
    return x * lax.rsqrt(jnp.mean(x * x, axis=-1, keepdims=True) + EPS) * g


def _rope_tables(t_len):
    rows = t_len // GRID_W
    row = np.repeat(np.arange(rows, dtype=np.float64), GRID_W)
    col = np.tile(np.arange(GRID_W, dtype=np.float64), rows)
    inv = ROPE_THETA ** (-np.arange(0, ROPE_AXIS_DIM, 2, dtype=np.float64) / ROPE_AXIS_DIM)
    ar = row[:, None] * inv
    ac = col[:, None] * inv
    cos = np.concatenate([np.cos(ar), np.cos(ar), np.cos(ac), np.cos(ac)], axis=1)
    sin = np.concatenate([-np.sin(ar), np.sin(ar), -np.sin(ac), np.sin(ac)], axis=1)
    return jnp.asarray(cos, F32), jnp.asarray(sin, F32)


def _rope(x, cos, sin):
    lane = lax.broadcasted_iota(jnp.int32, x.shape, 1)
    low = (lane % ROPE_AXIS_DIM) < (ROPE_AXIS_DIM // 2)
    partner = jnp.where(low, pltpu.roll(x, HEAD_DIM - ROPE_AXIS_DIM // 2, 1), pltpu.roll(x, ROPE_AXIS_DIM // 2, 1))
    return x * cos + partner * sin


Q_SCALE = HEAD_DIM ** -0.5 * math.log2(math.e)


def _with_ones(v):
    lane = lax.broadcasted_iota(jnp.int32, v.shape, 1)
    return jnp.concatenate([v, jnp.where(lane == 0, 1.0, 0.0).astype(v.dtype)], axis=1)


def _softmax_pv(q, kb, vb1):
    s = _dot_nt((q * Q_SCALE).astype(BF16), kb)
    p = jnp.exp2(s - jnp.max(s, axis=-1, keepdims=True)).astype(BF16)
    ov = _dot(p, vb1)
    return ov[:, :HEAD_DIM] / ov[:, HEAD_DIM:HEAD_DIM + 1]


def _attn_ctx_kernel(q_ref, k_ref, v_ref, qn_ref, kn_ref, o_ref, ko_ref, vo_ref):
    for kv in range(N_KV_HEADS):
        kcols = slice(kv * HEAD_DIM, (kv + 1) * HEAD_DIM)
        k = _head_rms(k_ref[:, kcols], kn_ref[...])
        v = v_ref[:, kcols]
        ko_ref[:, kcols] = k
        vo_ref[:, kcols] = v
        kb = k.astype(BF16)
        vb = _with_ones(v.astype(BF16))
        for g in range(N_GROUP):
            head = kv * N_GROUP + g
            cols = slice(head * HEAD_DIM, (head + 1) * HEAD_DIM)
            q = _head_rms(q_ref[:, cols], qn_ref[...])
            o_ref[:, cols] = _softmax_pv(q, kb, vb).astype(o_ref.dtype)


def _attn_ctx(za, q_norm, k_norm, layer):
    return pl.pallas_call(
        _attn_ctx_kernel,
        grid=(BATCH,),
        in_specs=[
            pl.BlockSpec((SEQ, ATTN_DIM), lambda b: (b, W_Q // ATTN_DIM)),
            pl.BlockSpec((SEQ, KV_DIM), lambda b: (b, W_K // KV_DIM)),
            pl.BlockSpec((SEQ, KV_DIM), lambda b: (b, W_V // KV_DIM)),
            pl.BlockSpec((None, 1, HEAD_DIM), lambda b: (layer, 0, 0)),
            pl.BlockSpec((None, 1, HEAD_DIM), lambda b: (layer, 0, 0)),
        ],
        out_specs=[
            pl.BlockSpec((SEQ, ATTN_DIM), lambda b: (b, 0)),
            pl.BlockSpec((None, SEQ, KV_DIM), lambda b: (b, 0, 0)),
            pl.BlockSpec((None, SEQ, KV_DIM), lambda b: (b, 0, 0)),
        ],
        out_shape=[
            jax.ShapeDtypeStruct((N_CTX, ATTN_DIM), BF16),
            jax.ShapeDtypeStruct((BATCH, SEQ, KV_DIM), F32),
            jax.ShapeDtypeStruct((BATCH, SEQ, KV_DIM), F32),
        ],
        compiler_params=_params(("arbitrary",), 24 << 20),
        name="attn_ctx",
    )(za, za, za, q_norm.reshape(DEPTH, 1, HEAD_DIM), k_norm.reshape(DEPTH, 1, HEAD_DIM))


def _attn_lat_kernel(q_ref, k_ref, v_ref, ck_ref, cv_ref, qn_ref, kn_ref, cosq_ref, sinq_ref,
                     cosk_ref, sink_ref, o_ref, kb_ref, vb_ref):
    @pl.when(pl.program_id(2) == 0)
    def _():
        kb_ref[0:PAST_LEN, :] = ck_ref[...].astype(BF16)
        vb_ref[0:PAST_LEN, :] = _with_ones(cv_ref[...].astype(BF16))
        k = _rope(_head_rms(k_ref[...], kn_ref[...]), cosk_ref[...], sink_ref[...])
        kb_ref[PAST_LEN:, :] = k.astype(BF16)
        vb_ref[PAST_LEN:, :] = _with_ones(v_ref[...].astype(BF16))

    kb = kb_ref[...]
    vb = vb_ref[...]
    for g in range(N_GROUP):
        cols = slice(g * HEAD_DIM, (g + 1) * HEAD_DIM)
        q = _rope(_head_rms(q_ref[:, cols], qn_ref[...]), cosq_ref[...], sinq_ref[...])
        o_ref[:, cols] = _softmax_pv(q, kb, vb).astype(o_ref.dtype)


def _attn_lat(za, cache_k, cache_v, q_norm, k_norm, layer):
    gw = N_GROUP * HEAD_DIM
    tq = 256
    nq = DEC_SEQ // tq
    cos, sin = _rope_tables(DEC_SEQ)
    ck = cache_k.reshape(DEC_BATCH, DEPTH, PAST_LEN, KV_DIM)
    cv = cache_v.reshape(DEC_BATCH, DEPTH, PAST_LEN, KV_DIM)
    seq0 = N_CTX // DEC_SEQ
    tile0 = N_CTX // tq
    return pl.pallas_call(
        _attn_lat_kernel,
        grid=(DEC_BATCH, N_KV_HEADS, nq),
        in_specs=[
            pl.BlockSpec((tq, gw), lambda b, h, i: (tile0 + b * nq + i, W_Q // gw + h)),
            pl.BlockSpec((DEC_SEQ, HEAD_DIM), lambda b, h, i: (seq0 + b, W_K // HEAD_DIM + h)),
            pl.BlockSpec((DEC_SEQ, HEAD_DIM), lambda b, h, i: (seq0 + b, W_V // HEAD_DIM + h)),
            pl.BlockSpec((None, None, PAST_LEN, HEAD_DIM), lambda b, h, i: (b, layer, 0, h)),
            pl.BlockSpec((None, None, PAST_LEN, HEAD_DIM), lambda b, h, i: (b, layer, 0, h)),
            pl.BlockSpec((None, 1, HEAD_DIM), lambda b, h, i: (layer, 0, 0)),
            pl.BlockSpec((None, 1, HEAD_DIM), lambda b, h, i: (layer, 0, 0)),
            pl.BlockSpec((tq, HEAD_DIM), lambda b, h, i: (i, 0)),
            pl.BlockSpec((tq, HEAD_DIM), lambda b, h, i: (i, 0)),
            pl.BlockSpec((DEC_SEQ, HEAD_DIM), lambda b, h, i: (0, 0)),
            pl.BlockSpec((DEC_SEQ, HEAD_DIM), lambda b, h, i: (0, 0)),
        ],
        out_specs=pl.BlockSpec((tq, gw), lambda b, h, i: (b * nq + i, h)),
        out_shape=jax.ShapeDtypeStruct((N_LAT, ATTN_DIM), BF16),
        scratch_shapes=[pltpu.VMEM((PAST_LEN + DEC_SEQ, HEAD_DIM), BF16),
                        pltpu.VMEM((PAST_LEN + DEC_SEQ, 2 * HEAD_DIM), BF16)],
        compiler_params=_params(("arbitrary", "arbitrary", "arbitrary"), 32 << 20),
        name="attn_lat",
    )(za, za, za, ck, cv, q_norm.reshape(DEPTH, 1, HEAD_DIM), k_norm.reshape(DEPTH, 1, HEAD_DIM),
      cos, sin, cos, sin)


def _gla_consts():
    c = GLA_CHUNK
    t = np.arange(c)
    cum_f = (t[None, :] <= t[:, None]).astype(np.float32)
    cum_b = (t[None, :] >= t[:, None]).astype(np.float32)
    upper, same = [], []
    for level in range(GLA_LEVELS):
        n = c >> level
        blk = t // n
        p = blk * n + n // 2
        upper.append(np.broadcast_to((t >= p).astype(np.float32)[:, None], (c, GLA_DK)))
        same.append((blk[:, None] == blk[None, :]).astype(np.float32))
    same.append(2.0 * np.eye(c, dtype=np.float32))
    return (jnp.asarray(cum_f, BF16), jnp.asarray(cum_b, BF16),
            jnp.asarray(np.stack(upper + [1.0 - u for u in upper]), F32), jnp.asarray(np.stack(same), F32))


def _chunk_select(t_len):
    nc = t_len // GLA_CHUNK
    sel = (np.arange(t_len)[:, None] // GLA_CHUNK == np.arange(LANES)[None, :]).astype(np.float32)
    assert nc <= LANES
    return jnp.asarray(sel, BF16)


def _split_hi_lo(x):
    hi = x.astype(BF16)
    lo = (x - hi.astype(F32)).astype(BF16)
    return jnp.concatenate([hi, lo], axis=1)


def _pivot_rows(b, level):
    c = GLA_CHUNK
    n = c >> level
    if n >= 16:
        parts = [jnp.broadcast_to(b[s + n // 2:s + n // 2 + 1, :], (n, GLA_DK)) for s in range(0, c, n)]
        return parts[0] if len(parts) == 1 else jnp.concatenate(parts, axis=0)
    b3 = b.reshape(c // 8, 8, GLA_DK)
    if n == 8:
        return jnp.broadcast_to(b3[:, 4:5, :], b3.shape).reshape(c, GLA_DK)
    if n == 4:
        sub = lax.broadcasted_iota(jnp.int32, b3.shape, 1)
        lo = jnp.broadcast_to(b3[:, 2:3, :], b3.shape)
        hi = jnp.broadcast_to(b3[:, 6:7, :], b3.shape)
        return jnp.where(sub < 4, lo, hi).reshape(c, GLA_DK)
    row = lax.broadcasted_iota(jnp.int32, b.shape, 0)
    return jnp.where((row & 1) == 0, pltpu.roll(b, c - 1, 0), b)


def _gla_decay_stage(h, ci, q_ref, k_ref, bf_ref, bb_ref, up_ref, qc_ref, slot):
    qt_ref, kt_ref, kd_ref = slot
    c = GLA_CHUNK
    rows = slice(ci * c, (ci + 1) * c)
    hk = slice(h * GLA_DK, (h + 1) * GLA_DK)
    q = q_ref[rows, hk] * (GLA_DK ** -0.5)
    k = k_ref[rows, hk]
    b_f = bf_ref[rows, hk]
    b_b = bb_ref[rows, hk]
    for level in range(GLA_LEVELS):
        wf = jnp.exp(-jnp.abs(b_f - _pivot_rows(b_f, level)))
        wb = jnp.exp(-jnp.abs(b_b - _pivot_rows(b_b, level)))
        up = up_ref[level]
        dn = up_ref[GLA_LEVELS + level]
        lv = slice(c * level, c * (level + 1))
        qt_ref[lv, :] = jnp.concatenate([q * (wf * up), q * (wb * dn)], axis=1).astype(BF16)
        kt_ref[lv, :] = jnp.concatenate([k * (wf * dn), k * (wb * up)], axis=1).astype(BF16)
    qc_ref[rows, h * GLA_DV:(h + 1) * GLA_DV] = jnp.concatenate(
        [q * jnp.exp(b_f), q * jnp.exp(b_b)], axis=1).astype(BF16)
    kd_ref[...] = jnp.concatenate([k * jnp.exp(b_f[c - 1:c, :] - b_f),
                                   k * jnp.exp(b_b[0:1, :] - b_b)], axis=1).astype(BF16)


def _head_cols(refs, h):
    width = refs[0].shape[1]
    start = h * GLA_DV
    return refs[start // width], slice(start % width, start % width + GLA_DV)


def _gla_matmul_stage(h, ci, nc, q_ref, k_ref, v_refs, same_ref, oi_ref, kv_ref, slot):
    qt_ref, kt_ref, kd_ref = slot
    c = GLA_CHUNK
    rows = slice(ci * c, (ci + 1) * c)
    hk = slice(h * GLA_DK, (h + 1) * GLA_DK)
    q = (q_ref[rows, hk] * (GLA_DK ** -0.5)).astype(BF16)
    k = k_ref[rows, hk].astype(BF16)
    v_ref, vcols = _head_cols(v_refs, h)
    v = v_ref[rows, vcols].astype(BF16)
    att = _dot_nt(q, k) * same_ref[GLA_LEVELS]
    for level in range(GLA_LEVELS):
        lv = slice(c * level, c * (level + 1))
        att = att + _dot_nt(qt_ref[lv, :], kt_ref[lv, :]) * same_ref[level]
    oi_ref[rows, h * GLA_DV:(h + 1) * GLA_DV] = _dot(att.astype(BF16), v)
    kv_ref[h * nc + ci] = _dot_tn(kd_ref[...], v)


def _gla_kernel(*refs, t_len, hps, nv, has_state):
    refs = list(refs)
    q_ref, k_ref = refs[0:2]
    v_refs = refs[2:2 + nv]
    r_refs = refs[2 + nv:2 + 2 * nv]
    (zr_ref, wf_ref, wb_ref, bf_in_ref, bb_in_ref, gn_ref,
     cumf_ref, cumb_ref, up_ref, same_ref, sel_ref) = refs[2 + 2 * nv:13 + 2 * nv]
    rest = refs[13 + 2 * nv:]
    if has_state:
        s0f_ref, s0b_ref = rest[0:2]
        rest = rest[2:]
    (o_ref, sf_ref, sb_ref, bf_ref, bb_ref, oi_ref, qc_ref, kv_ref, st_ref,
     qt0, kt0, kd0, qt1, kt1, kd1) = rest
    c = GLA_CHUNK
    nc = t_len // c
    width = hps * GLA_DK
    slots = ((qt0, kt0, kd0), (qt1, kt1, kd1))
    zr = zr_ref[...].astype(BF16)
    laf = _log_sigmoid(_dot(zr, wf_ref[...]) + bf_in_ref[...]) * (1.0 / GLA_GATE_TAU)
    lab = _log_sigmoid(_dot(zr, wb_ref[...]) + bb_in_ref[...]) * (1.0 / GLA_GATE_TAU)
    bf_ref[...] = laf
    bb_ref[...] = lab

    def chunk_totals(la):
        s = _dot_tn(_split_hi_lo(la), sel_ref[...])
        return jnp.exp(s[:width, :] + s[width:, :])

    tot_f = chunk_totals(laf)
    tot_b = chunk_totals(lab)

    for ci in range(nc):
        rows = slice(ci * c, (ci + 1) * c)
        sf = _dot(cumf_ref[...], _split_hi_lo(bf_ref[rows, :]))
        sb = _dot(cumb_ref[...], _split_hi_lo(bb_ref[rows, :]))
        bf_ref[rows, :] = sf[:, :width] + sf[:, width:]
        bb_ref[rows, :] = sb[:, :width] + sb[:, width:]

    decay = functools.partial(_gla_decay_stage, q_ref=q_ref, k_ref=k_ref, bf_ref=bf_ref, bb_ref=bb_ref,
                              up_ref=up_ref, qc_ref=qc_ref)
    matmuls = functools.partial(_gla_matmul_stage, nc=nc, q_ref=q_ref, k_ref=k_ref, v_refs=v_refs,
                                same_ref=same_ref, oi_ref=oi_ref, kv_ref=kv_ref)
    items = [(h, ci) for h in range(hps) for ci in range(nc)]
    decay(*items[0], slot=slots[0])
    for n, item in enumerate(items):
        if n + 1 < len(items):
            decay(*items[n + 1], slot=slots[(n + 1) % 2])
        matmuls(*item, slot=slots[n % 2])

    for h in range(hps):
        hk = slice(h * GLA_DK, (h + 1) * GLA_DK)
        s = s0f_ref[...] if has_state else jnp.zeros((GLA_DK, GLA_DV), F32)
        for ci in range(nc):
            st_ref[h * nc + ci, 0:GLA_DK, :] = s.astype(BF16)
            s = tot_f[hk, ci:ci + 1] * s + kv_ref[h * nc + ci, 0:GLA_DK, :]
        sf_ref[h] = s
        s = s0b_ref[...] if has_state else jnp.zeros((GLA_DK, GLA_DV), F32)
        for ci in reversed(range(nc)):
            st_ref[h * nc + ci, GLA_DK:2 * GLA_DK, :] = s.astype(BF16)
            s = tot_b[hk, ci:ci + 1] * s + kv_ref[h * nc + ci, GLA_DK:2 * GLA_DK, :]
        sb_ref[h] = s

    for h, ci in items:
        rows = slice(ci * c, (ci + 1) * c)
        hv = slice(h * GLA_DV, (h + 1) * GLA_DV)
        o = oi_ref[rows, hv] + _dot(qc_ref[rows, hv], st_ref[h * nc + ci])
        o = o * lax.rsqrt(jnp.mean(o * o, axis=-1, keepdims=True) + EPS) * gn_ref[...]
        r_ref, rcols = _head_cols(r_refs, h)
        r = r_ref[rows, rcols]
        o_ref[rows, hv] = (o * (r * _sigmoid(r))).astype(o_ref.dtype)


def _gla(za, wgf, wgb, b_gate_f, b_gate_b, gla_norm, layer, row0, n_seq, t_len, hps, s0f=None, s0b=None):
    has_state = s0f is not None
    assert not has_state or hps == 1
    cum_f, cum_b, upper, same = _gla_consts()
    sel = _chunk_select(t_len)
    seq0 = row0 // t_len
    nc = t_len // GLA_CHUNK
    width = hps * GLA_DK
    vw = min(hps, 2) * GLA_DV
    nv = hps * GLA_DV // vw
    const = lambda shape: pl.BlockSpec(shape, lambda b, h: (0,) * len(shape))
    wide = lambda col0, j: pl.BlockSpec((t_len, vw), lambda b, h: (seq0 + b, col0 // vw + h * nv + j))
    in_specs = (
        [pl.BlockSpec((t_len, width), lambda b, h: (seq0 + b, W_GQ // width + h)),
         pl.BlockSpec((t_len, width), lambda b, h: (seq0 + b, W_GK // width + h))]
        + [wide(W_GV, j) for j in range(nv)] + [wide(W_GR, j) for j in range(nv)]
        + [pl.BlockSpec((t_len, LANES), lambda b, h: (seq0 + b, W_RANK // LANES)),
           pl.BlockSpec((None, LANES, width), lambda b, h: (layer, 0, h)),
           pl.BlockSpec((None, LANES, width), lambda b, h: (layer, 0, h)),
           pl.BlockSpec((None, 1, width), lambda b, h: (layer, 0, h)),
           pl.BlockSpec((None, 1, width), lambda b, h: (layer, 0, h)),
           pl.BlockSpec((None, 1, GLA_DV), lambda b, h: (layer, 0, 0)),
           const(cum_f.shape), const(cum_b.shape), const(upper.shape), const(same.shape), const(sel.shape)])
    args = [za] * (3 + 2 * nv) + [wgf, wgb, b_gate_f.reshape(DEPTH, 1, GLA_K_DIM),
                                  b_gate_b.reshape(DEPTH, 1, GLA_K_DIM), gla_norm.reshape(DEPTH, 1, GLA_DV),
                                  cum_f, cum_b, upper, same, sel]
    if has_state:
        in_specs += [
            pl.BlockSpec((None, None, None, GLA_DK, GLA_DV), lambda b, h: (b, layer, h, 0, 0)),
            pl.BlockSpec((None, None, None, GLA_DK, GLA_DV), lambda b, h: (b, layer, h, 0, 0)),
        ]
        args += [s0f, s0b]
    return pl.pallas_call(
        functools.partial(_gla_kernel, t_len=t_len, hps=hps, nv=nv, has_state=has_state),
        grid=(n_seq, GLA_HEADS // hps),
        in_specs=in_specs,
        out_specs=[
            pl.BlockSpec((t_len, hps * GLA_DV), lambda b, h: (b, h)),
            pl.BlockSpec((None, hps, GLA_DK, GLA_DV), lambda b, h: (b, h, 0, 0)),
            pl.BlockSpec((None, hps, GLA_DK, GLA_DV), lambda b, h: (b, h, 0, 0)),
        ],
        out_shape=[
            jax.ShapeDtypeStruct((n_seq * t_len, GLA_V_DIM), BF16),
            jax.ShapeDtypeStruct((n_seq, GLA_HEADS, GLA_DK, GLA_DV), F32),
            jax.ShapeDtypeStruct((n_seq, GLA_HEADS, GLA_DK, GLA_DV), F32),
        ],
        scratch_shapes=[
            pltpu.VMEM((t_len, width), F32), pltpu.VMEM((t_len, width), F32),
            pltpu.VMEM((t_len, hps * GLA_DV), F32), pltpu.VMEM((t_len, hps * GLA_DV), BF16),
            pltpu.VMEM((hps * nc, 2 * GLA_DK, GLA_DV), F32), pltpu.VMEM((hps * nc, 2 * GLA_DK, GLA_DV), BF16),
        ] + 2 * [pltpu.VMEM((GLA_CHUNK * GLA_LEVELS, 2 * GLA_DK), BF16),
                 pltpu.VMEM((GLA_CHUNK * GLA_LEVELS, 2 * GLA_DK), BF16),
                 pltpu.VMEM((GLA_CHUNK, 2 * GLA_DK), BF16)],
        compiler_params=_params(("arbitrary", "arbitrary"), 40 << 20),
        name="gla_%d" % t_len,
    )(*args)


def _merge_kernel(fc_ref, fl_ref, ac_ref, al_ref, gc_ref, gl_ref, wa_ref, wb_ref, wc_ref,
                  ga_ref, gb_ref, gg_ref, o_ref, wa_s, wb_s, wc_s, *, ctx_tiles):
    i = pl.program_id(1)

    @pl.when(i == 0)
    def _():
        wa_s[...] = wa_ref[...].astype(BF16)
        wb_s[...] = wb_ref[...].astype(BF16)
        wc_s[...] = wc_ref[...].astype(BF16)

    def compute(f_ref, a_ref, g_ref):
        acc = _sigmoid(ga_ref[...].astype(F32)) * _dot(f_ref[...], wa_s[...])
        acc = acc + _sigmoid(gb_ref[...].astype(F32)) * _dot(a_ref[...], wb_s[...])
        acc = acc + _sigmoid(gg_ref[...].astype(F32)) * _dot(g_ref[...], wc_s[...])
        o_ref[...] = acc.astype(o_ref.dtype)

    @pl.when(i < ctx_tiles)
    def _():
        compute(fc_ref, ac_ref, gc_ref)

    @pl.when(i >= ctx_tiles)
    def _():
        compute(fl_ref, al_ref, gl_ref)


def _merge(fa, at, gl, w_fourier, w_attn, w_gla, zg, layer):
    tm, tn = 256, 1024
    k = F_DIM
    nj = D_MODEL // tn
    ctx_tiles = N_CTX // tm
    ctx = pl.BlockSpec((tm, k), lambda j, i: (jnp.minimum(i, ctx_tiles - 1), 0))
    lat = pl.BlockSpec((tm, k), lambda j, i: (jnp.maximum(i - ctx_tiles, 0), 0))
    wsp = pl.BlockSpec((None, k, tn), lambda j, i: (layer, 0, j))
    gate = lambda br: pl.BlockSpec((tm, tn), lambda j, i: (i, br * nj + j))
    return pl.pallas_call(
        functools.partial(_merge_kernel, ctx_tiles=ctx_tiles),
        grid=(nj, N_TOK // tm),
        in_specs=[ctx, lat, ctx, lat, ctx, lat, wsp, wsp, wsp, gate(0), gate(1), gate(2)],
        out_specs=pl.BlockSpec((tm, tn), lambda j, i: (i, j)),
        out_shape=jax.ShapeDtypeStruct((N_TOK, D_MODEL), BF16),
        scratch_shapes=[pltpu.VMEM((k, tn), BF16)] * 3,
        compiler_params=_params(("arbitrary", "arbitrary"),
                                2 * (6 * tm * k * 2 + 3 * k * tn * 4 + 3 * tm * tn * 4 + tm * tn * 2)
                                + 3 * k * tn * 2 + 4 * tm * tn * 4 + (8 << 20)),
        name="merge",
    )(fa[0], fa[1], at[0], at[1], gl[0], gl[1], w_fourier, w_attn, w_gla, zg, zg, zg)


def _convffn_up_kernel(h_ref, wg_ref, wv_ref, cwg_ref, cwv_ref, cbg_ref, cbv_ref, o_ref, wg_s, wv_s, *, tm):
    @pl.when(pl.program_id(1) == 0)
    def _():
        wg_s[...] = wg_ref[...].astype(BF16)
        wv_s[...] = wv_ref[...].astype(BF16)

    row0 = pl.program_id(1) * tm
    seq_len = jnp.where(row0 < N_CTX, SEQ, DEC_SEQ)
    h = h_ref[...]
    pos = lax.broadcasted_iota(jnp.int32, (tm, 1), 0) & (seq_len - 1)
    has_prev = (pos != 0).astype(F32)
    has_next = (pos != seq_len - 1).astype(F32)

    def conv(u, cw_ref, cb_ref):
        prev = pltpu.roll(u, 1, 0) * has_prev
        nxt = pltpu.roll(u, tm - 1, 0) * has_next
        return prev * cw_ref[0:1, :] + u * cw_ref[1:2, :] + nxt * cw_ref[2:3, :] + cb_ref[...]

    g = conv(_dot(h, wg_s[...]), cwg_ref, cbg_ref)
    val = conv(_dot(h, wv_s[...]), cwv_ref, cbv_ref)
    o_ref[...] = (g * _sigmoid(g) * val).astype(o_ref.dtype)


def _convffn_up(h, w_up, conv_w, conv_b, layer):
    tm, tn = ROW_TILE, 512
    k = D_MODEL
    nj = D_FF // tn
    cb = conv_b.reshape(DEPTH, 1, 2 * D_FF)
    return pl.pallas_call(
        functools.partial(_convffn_up_kernel, tm=tm),
        grid=(nj, N_TOK // tm),
        in_specs=[
            pl.BlockSpec((tm, k), lambda j, i: (i, 0)),
            pl.BlockSpec((None, k, tn), lambda j, i: (layer, 0, j)),
            pl.BlockSpec((None, k, tn), lambda j, i: (layer, 0, nj + j)),
            pl.BlockSpec((None, 3, tn), lambda j, i: (layer, 0, j)),
            pl.BlockSpec((None, 3, tn), lambda j, i: (layer, 0, nj + j)),
            pl.BlockSpec((None, 1, tn), lambda j, i: (layer, 0, j)),
            pl.BlockSpec((None, 1, tn), lambda j, i: (layer, 0, nj + j)),
        ],
        out_specs=pl.BlockSpec((tm, tn), lambda j, i: (i, j)),
        out_shape=jax.ShapeDtypeStruct((N_TOK, D_FF), BF16),
        scratch_shapes=[pltpu.VMEM((k, tn), BF16)] * 2,
        compiler_params=_params(("arbitrary", "arbitrary"),
                                2 * (tm * k * 2 + 2 * k * tn * 4 + tm * tn * 2) + 2 * k * tn * 2
                                + 8 * tm * tn * 4 + (8 << 20)),
        name="convffn_up",
    )(h, w_up, w_up, conv_w, conv_w, cb, cb)


def _final_norm_kernel(x_ref, g_ref, oc_ref, ol_ref, *, ctx_tiles):
    i = pl.program_id(0)
    x = x_ref[...]
    y = x * lax.rsqrt(jnp.mean(x * x, axis=-1, keepdims=True) + EPS) * g_ref[...]

    @pl.when(i < ctx_tiles)
    def _():
        oc_ref[...] = y

    @pl.when(i >= ctx_tiles)
    def _():
        ol_ref[...] = y


def _final_norm(x, g):
    tm = 512
    ctx_tiles = N_CTX // tm
    return pl.pallas_call(
        functools.partial(_final_norm_kernel, ctx_tiles=ctx_tiles),
        grid=(N_TOK // tm,),
        in_specs=[pl.BlockSpec((tm, D_MODEL), lambda m: (m, 0)),
                  pl.BlockSpec((1, D_MODEL), lambda m: (0, 0))],
        out_specs=[pl.BlockSpec((tm, D_MODEL), lambda m: (jnp.minimum(m, ctx_tiles - 1), 0)),
                   pl.BlockSpec((tm, D_MODEL), lambda m: (jnp.maximum(m - ctx_tiles, 0), 0))],
        out_shape=[jax.ShapeDtypeStruct((N_CTX, D_MODEL), F32), jax.ShapeDtypeStruct((N_LAT, D_MODEL), F32)],
        compiler_params=_params(("arbitrary",), 8 * tm * D_MODEL * 4 + (8 << 20)),
        name="final_norm",
    )(x, g.reshape(1, D_MODEL))


def _pad_gate_w(w_gate, row0):
    out = jnp.zeros((DEPTH, LANES, GLA_K_DIM), BF16)
    return out.at[:, row0:row0 + GLA_GATE_RANK, :].set(w_gate.astype(BF16))


def kernel(x_prompt, x_sample, cache_k, cache_v, state_gla_fwd, state_gla_bwd, c, c_ctx, w_ada, b_ada, norm1, w_in, q_norm, k_norm, w_fourier, w_attn, w_gate_f, b_gate_f, w_gate_b, b_gate_b, gla_norm, w_gla, w_out, norm2, w_up, conv_w, conv_b, w_down, final_norm):
    x = jnp.concatenate([x_prompt.reshape(N_CTX, D_MODEL), x_sample.reshape(N_LAT, D_MODEL)], axis=0)
    cvec = jnp.concatenate([c_ctx[None, :], c, jnp.zeros((8 - 1 - DEC_BATCH, D_MODEL), F32)], axis=0)
    mod = _ada(cvec, w_ada, b_ada).reshape(DEPTH, 8, 1, N_MOD * D_MODEL)
    w_t = jnp.swapaxes(w_in, 1, 2)
    wgf = _pad_gate_w(w_gate_f, 0)
    wgb = _pad_gate_w(w_gate_b, GLA_GATE_RANK)

    new_k, new_v, new_sf, new_sb = [], [], [], []
    for l in range(DEPTH):
        h = _modnorm(x, norm1, mod, l, 0, 1)
        za = _in_proj(h, w_t, l)
        zg = _gates_proj(h, w_t, l)
        fa = (_fnet(za, 0, BATCH, SEQ), _fnet(za, N_CTX, DEC_BATCH, DEC_SEQ))
        at_ctx, k_ctx, v_ctx = _attn_ctx(za, q_norm, k_norm, l)
        at = (at_ctx, _attn_lat(za, cache_k, cache_v, q_norm, k_norm, l))
        gl_ctx, sf, sb = _gla(za, wgf, wgb, b_gate_f, b_gate_b, gla_norm, l, 0, BATCH, SEQ, GLA_HEADS)
        gl_lat, _, _ = _gla(za, wgf, wgb, b_gate_f, b_gate_b, gla_norm, l, N_CTX, DEC_BATCH, DEC_SEQ, 1,
                            state_gla_fwd, state_gla_bwd)
        merged = _merge(fa, at, (gl_ctx, gl_lat), w_fourier, w_attn, w_gla, zg, l)
        x = _resid_proj(merged, w_out, x, mod, l, 2, ROW_TILE, 1024)
        h = _modnorm(x, norm2, mod, l, 3, 4)
        hmid = _convffn_up(h, w_up, conv_w, conv_b, l)
        x = _resid_proj(hmid, w_down, x, mod, l, 5, 512, 512)
        new_k.append(k_ctx)
        new_v.append(v_ctx)
        new_sf.append(sf)
        new_sb.append(sb)

    y_ctx, y_lat = _final_norm(x, final_norm)
    y_prompt = y_ctx.reshape(BATCH, SEQ, D_MODEL)
    y_sample = y_lat.reshape(DEC_BATCH, DEC_SEQ, D_MODEL)
    kv_shape = (BATCH, DEPTH, SEQ, N_KV_HEADS, HEAD_DIM)
    return (y_prompt, y_sample,
            jnp.stack(new_k, axis=1).reshape(kv_shape), jnp.stack(new_v, axis=1).reshape(kv_shape),
            jnp.stack(new_sf, axis=1), jnp.stack(new_sb, axis=1))
```

```python
import functools
import math

import numpy as np
import jax
import jax.numpy as jnp
from jax import lax
from jax.experimental import pallas as pl
from jax.experimental.pallas import tpu as pltpu

F32 = jnp.float32
BF16 = jnp.bfloat16

D_MODEL = 2048
BATCH = 16
SEQ = 256
DEPTH = 4
DEC_BATCH = 4
DEC_SEQ = 1024
PAST_LEN = 256
GRID_W = 64
HEAD_DIM = 128
N_Q_HEADS = 8
N_KV_HEADS = 2
N_GROUP = N_Q_HEADS // N_KV_HEADS
ATTN_DIM = N_Q_HEADS * HEAD_DIM
KV_DIM = N_KV_HEADS * HEAD_DIM
ROPE_THETA = 10000.0
ROPE_AXIS_DIM = HEAD_DIM // 2
F_GROUPS = 8
F_GROUP_DIM = 128
F_DIM = F_GROUPS * F_GROUP_DIM
GLA_HEADS = 4
GLA_DK = 128
GLA_DV = 256
GLA_K_DIM = GLA_HEADS * GLA_DK
GLA_V_DIM = GLA_HEADS * GLA_DV
GLA_GATE_RANK = 16
GLA_GATE_TAU = 16.0
D_FF = 5632
N_MOD = 6
N_BRANCH = 3
EPS = 1e-6

N_CTX = BATCH * SEQ
N_LAT = DEC_BATCH * DEC_SEQ
N_TOK = N_CTX + N_LAT

VMEM_CAP_BYTES = 60 * 1024 * 1024
LANES = 128

W_FIN = 0
W_Q = W_FIN + F_DIM
W_K = W_Q + ATTN_DIM
W_V = W_K + KV_DIM
W_GQ = W_V + KV_DIM
W_GK = W_GQ + GLA_K_DIM
W_GV = W_GK + GLA_K_DIM
W_GR = W_GV + GLA_V_DIM
W_RANK = W_GR + GLA_V_DIM
W_GATES = W_RANK + 2 * GLA_GATE_RANK
N_IN = W_GATES + N_BRANCH * D_MODEL

GLA_CHUNK = 64
GLA_LEVELS = 6
ROW_TILE = 1024


def _params(semantics, vmem_bytes):
    return pltpu.CompilerParams(dimension_semantics=semantics,
                                vmem_limit_bytes=int(min(vmem_bytes, VMEM_CAP_BYTES)))


def _mod_row(row_start):
    return jnp.where(row_start < N_CTX, 0, 1 + (row_start - N_CTX) // DEC_SEQ)


def _sigmoid(x):
    return 1.0 / (1.0 + jnp.exp(-x))


def _log_sigmoid(x):
    return jnp.minimum(x, 0.0) - jnp.log(1.0 + jnp.exp(-jnp.abs(x)))


def _dot(a, b):
    return jnp.dot(a, b, preferred_element_type=F32)


def _dot_nt(a, b):
    return lax.dot_general(a, b, (((1,), (1,)), ((), ())), preferred_element_type=F32)


def _dot_tn(a, b):
    return lax.dot_general(a, b, (((0,), (0,)), ((), ())), preferred_element_type=F32)


def _ada_kernel(c_ref, w_ref, b_ref, o_ref):
    c = c_ref[...]
    a = (c * _sigmoid(c)).astype(BF16)
    o_ref[...] = _dot(a, w_ref[...].astype(BF16)) + b_ref[...]


def _ada(cvec, w_ada, b_ada):
    tn = 2048
    n = N_MOD * D_MODEL
    return pl.pallas_call(
        _ada_kernel,
        grid=(DEPTH, n // tn),
        in_specs=[
            pl.BlockSpec((8, D_MODEL), lambda l, j: (0, 0)),
            pl.BlockSpec((None, D_MODEL, tn), lambda l, j: (l, 0, j)),
            pl.BlockSpec((None, 1, tn), lambda l, j: (l, 0, j)),
        ],
        out_specs=pl.BlockSpec((None, 8, tn), lambda l, j: (l, 0, j)),
        out_shape=jax.ShapeDtypeStruct((DEPTH, 8, n), F32),
        compiler_params=_params(("arbitrary", "arbitrary"), 3 * D_MODEL * tn * 4 + (8 << 20)),
        name="ada_mod",
    )(cvec, w_ada, b_ada.reshape(DEPTH, 1, n))


def _modnorm_kernel(x_ref, g_ref, shift_ref, scale_ref, o_ref):
    x = x_ref[...]
    y = x * lax.rsqrt(jnp.mean(x * x, axis=-1, keepdims=True) + EPS) * g_ref[...]
    o_ref[...] = (y * (1.0 + scale_ref[...]) + shift_ref[...]).astype(o_ref.dtype)


def _modnorm(x, gains, mod, layer, shift_col, scale_col):
    tm = ROW_TILE
    return pl.pallas_call(
        _modnorm_kernel,
        grid=(N_TOK // tm,),
        in_specs=[
            pl.BlockSpec((tm, D_MODEL), lambda m: (m, 0)),
            pl.BlockSpec((None, 1, D_MODEL), lambda m: (layer, 0, 0)),
            pl.BlockSpec((None, None, 1, D_MODEL), lambda m: (layer, _mod_row(m * tm), 0, shift_col)),
            pl.BlockSpec((None, None, 1, D_MODEL), lambda m: (layer, _mod_row(m * tm), 0, scale_col)),
        ],
        out_specs=pl.BlockSpec((tm, D_MODEL), lambda m: (m, 0)),
        out_shape=jax.ShapeDtypeStruct((N_TOK, D_MODEL), BF16),
        compiler_params=_params(("arbitrary",), 6 * tm * D_MODEL * 4 + (8 << 20)),
        name="modnorm",
    )(x, gains.reshape(DEPTH, 1, D_MODEL), mod, mod)


IN_TILE = 1024
ZA_COLS = 6 * IN_TILE
HALF_TILE = IN_TILE // 2
GATE_SHIFT = W_GATES - W_RANK


def _in_proj_kernel(a_ref, w_ref, o_ref, wbf_ref):
    @pl.when(pl.program_id(1) == 0)
    def _():
        wbf_ref[...] = w_ref[...].astype(BF16)

    o_ref[...] = _dot_nt(a_ref[...], wbf_ref[...])


def _in_proj(a, w_t, layer):
    m, k = a.shape
    tm, tn = ROW_TILE, IN_TILE
    return pl.pallas_call(
        _in_proj_kernel,
        grid=(ZA_COLS // tn, m // tm),
        in_specs=[
            pl.BlockSpec((tm, k), lambda j, i: (i, 0)),
            pl.BlockSpec((None, tn, k), lambda j, i: (layer, j, 0)),
        ],
        out_specs=pl.BlockSpec((tm, tn), lambda j, i: (i, j)),
        out_shape=jax.ShapeDtypeStruct((m, ZA_COLS), F32),
        scratch_shapes=[pltpu.VMEM((tn, k), BF16)],
        compiler_params=_params(("arbitrary", "arbitrary"),
                                2 * (tm * k * 2 + k * tn * 4 + tm * tn * 4) + k * tn * 2 + (8 << 20)),
        name="in_proj",
    )(a, w_t)


def _gates_kernel(a_ref, wa_ref, wb_ref, wx_ref, o_ref, wbf_ref):
    @pl.when(pl.program_id(1) == 0)
    def _():
        head = HALF_TILE - GATE_SHIFT
        wbf_ref[0:head, :] = wa_ref[GATE_SHIFT:, :].astype(BF16)
        wbf_ref[head:head + HALF_TILE, :] = wb_ref[...].astype(BF16)
        wbf_ref[head + HALF_TILE:, :] = wx_ref[...].astype(BF16)

    o_ref[...] = _dot_nt(a_ref[...], wbf_ref[...]).astype(o_ref.dtype)


def _gates_proj(a, w_t, layer):
    m, k = a.shape
    tm, tn = ROW_TILE, IN_TILE
    n = N_BRANCH * D_MODEL
    blk0 = W_RANK // HALF_TILE
    assert blk0 * HALF_TILE == W_RANK and HALF_TILE % GATE_SHIFT == 0
    xper = HALF_TILE // GATE_SHIFT
    return pl.pallas_call(
        _gates_kernel,
        grid=(n // tn, m // tm),
        in_specs=[
            pl.BlockSpec((tm, k), lambda j, i: (i, 0)),
            pl.BlockSpec((None, HALF_TILE, k), lambda j, i: (layer, blk0 + 2 * j, 0)),
            pl.BlockSpec((None, HALF_TILE, k), lambda j, i: (layer, blk0 + 2 * j + 1, 0)),
            pl.BlockSpec((None, GATE_SHIFT, k), lambda j, i: (layer, (blk0 + 2 * j + 2) * xper, 0)),
        ],
        out_specs=pl.BlockSpec((tm, tn), lambda j, i: (i, j)),
        out_shape=jax.ShapeDtypeStruct((m, n), BF16),
        scratch_shapes=[pltpu.VMEM((tn, k), BF16)],
        compiler_params=_params(("arbitrary", "arbitrary"),
                                2 * (tm * k * 2 + k * (tn + GATE_SHIFT) * 4 + tm * tn * 4) + k * tn * 2 + (8 << 20)),
        name="gates_proj",
    )(a, w_t, w_t, w_t)


def _resid_kernel(a_ref, w_ref, x_ref, gate_ref, o_ref, wbf_ref):
    @pl.when(pl.program_id(1) == 0)
    def _():
        wbf_ref[...] = w_ref[...].astype(BF16)

    o_ref[...] = x_ref[...] + gate_ref[...] * _dot(a_ref[...], wbf_ref[...])


def _resid_proj(a, w, x, mod, layer, gate_col, tm, tn, weight_buffers=2):
    m, k = a.shape
    n = D_MODEL
    gate_blk = gate_col * (D_MODEL // tn)
    w_mode = {} if weight_buffers == 2 else {"pipeline_mode": pl.Buffered(weight_buffers)}
    return pl.pallas_call(
        _resid_kernel,
        grid=(n // tn, m // tm),
        in_specs=[
            pl.BlockSpec((tm, k), lambda j, i: (i, 0)),
            pl.BlockSpec((None, k, tn), lambda j, i: (layer, 0, j), **w_mode),
            pl.BlockSpec((tm, tn), lambda j, i: (i, j)),
            pl.BlockSpec((None, None, 1, tn), lambda j, i: (layer, _mod_row(i * tm), 0, gate_blk + j)),
        ],
        out_specs=pl.BlockSpec((tm, tn), lambda j, i: (i, j)),
        out_shape=jax.ShapeDtypeStruct((m, n), F32),
        scratch_shapes=[pltpu.VMEM((k, tn), BF16)],
        compiler_params=_params(("arbitrary", "arbitrary"),
                                2 * (tm * k * 2 + 2 * tm * tn * 4) + weight_buffers * k * tn * 4 + k * tn * 2
                                + (8 << 20)),
        name="resid_proj",
    )(a, w, x, mod)


def _dft_consts(t_len):
    kc = np.arange(F_GROUP_DIM)
    ang_c = 2.0 * np.pi * ((kc[:, None] * kc[None, :]) % F_GROUP_DIM) / F_GROUP_DIM
    chan = np.concatenate([np.cos(ang_c), np.sin(ang_c)], axis=1)
    kt = np.arange(t_len)
    ang_t = 2.0 * np.pi * ((kt[:, None] * kt[None, :]) % t_len) / t_len
    pos = np.concatenate([np.cos(ang_t), -np.sin(ang_t)], axis=1)
    return jnp.asarray(chan, F32), jnp.asarray(pos, F32)


def _fnet_kernel(x_ref, chan_ref, pos_ref, o_ref, u_ref, *, t_len):
    @pl.when(pl.program_id(1) == 0)
    def _():
        chan = chan_ref[...].astype(BF16)
        for g in range(F_GROUPS):
            cols = slice(g * F_GROUP_DIM, (g + 1) * F_GROUP_DIM)
            cs = _dot(x_ref[:, cols].astype(BF16), chan)
            u_ref[0:t_len, cols] = cs[:, :F_GROUP_DIM].astype(BF16)
            u_ref[t_len:2 * t_len, cols] = cs[:, F_GROUP_DIM:].astype(BF16)

    y = _dot(pos_ref[...].astype(BF16), u_ref[...]) * (1.0 / math.sqrt(t_len * F_GROUP_DIM))
    o_ref[...] = y.astype(o_ref.dtype)


def _fnet(za, row0, n_seq, t_len):
    chan, pos = _dft_consts(t_len)
    tq = 256
    nq = t_len // tq
    seq_blk0 = row0 // t_len
    return pl.pallas_call(
        functools.partial(_fnet_kernel, t_len=t_len),
        grid=(n_seq, nq),
        in_specs=[
            pl.BlockSpec((t_len, F_DIM), lambda b, i: (seq_blk0 + b, W_FIN // F_DIM)),
            pl.BlockSpec((F_GROUP_DIM, 2 * F_GROUP_DIM), lambda b, i: (0, 0)),
            pl.BlockSpec((tq, 2 * t_len), lambda b, i: (i, 0)),
        ],
        out_specs=pl.BlockSpec((tq, F_DIM), lambda b, i: (b * nq + i, 0)),
        out_shape=jax.ShapeDtypeStruct((n_seq * t_len, F_DIM), BF16),
        scratch_shapes=[pltpu.VMEM((2 * t_len, F_DIM), BF16)],
        compiler_params=_params(("arbitrary", "arbitrary"),
                                2 * (t_len * F_DIM * 4 + tq * 2 * t_len * 4 + tq * F_DIM * 2)
                                + 2 * t_len * F_DIM * 2 + tq * 2 * t_len * 2 + tq * F_DIM * 8 + (8 << 20)),
        name="fnet_%d" % t_len,
    )(za, chan, pos)


def _head_rms(x, g):
    return x * lax.rsqrt(jnp.mean(x * x, axis=-1, keepdims=True) + EPS) * g


def _rope_tables(t_len):
    rows = t_len // GRID_W
    row = np.repeat(np.arange(rows, dtype=np.float64), GRID_W)
    col = np.tile(np.arange(GRID_W, dtype=np.float64), rows)
    inv = ROPE_THETA ** (-np.arange(0, ROPE_AXIS_DIM, 2, dtype=np.float64) / ROPE_AXIS_DIM)
    ar = row[:, None] * inv
    ac = col[:, None] * inv
    cos = np.concatenate([np.cos(ar), np.cos(ar), np.cos(ac), np.cos(ac)], axis=1)
    sin = np.concatenate([-np.sin(ar), np.sin(ar), -np.sin(ac), np.sin(ac)], axis=1)
    return jnp.asarray(cos, F32), jnp.asarray(sin, F32)


def _rope(x, cos, sin):
    lane = lax.broadcasted_iota(jnp.int32, x.shape, 1)
    low = (lane % ROPE_AXIS_DIM) < (ROPE_AXIS_DIM // 2)
    partner = jnp.where(low, pltpu.roll(x, HEAD_DIM - ROPE_AXIS_DIM // 2, 1), pltpu.roll(x, ROPE_AXIS_DIM // 2, 1))
    return x * cos + partner * sin


Q_SCALE = HEAD_DIM ** -0.5 * math.log2(math.e)


def _with_ones(v):
    lane = lax.broadcasted_iota(jnp.int32, v.shape, 1)
    return jnp.concatenate([v, jnp.where(lane == 0, 1.0, 0.0).astype(v.dtype)], axis=1)


def _softmax_pv(q, kb, vb1):
    s = _dot_nt((q * Q_SCALE).astype(BF16), kb)
    p = jnp.exp2(s - jnp.max(s, axis=-1, keepdims=True)).astype(BF16)
    ov = _dot(p, vb1)
    return ov[:, :HEAD_DIM] / ov[:, HEAD_DIM:HEAD_DIM + 1]


def _attn_ctx_kernel(q_ref, k_ref, v_ref, qn_ref, kn_ref, o_ref, ko_ref, vo_ref):
    for kv in range(N_KV_HEADS):
        kcols = slice(kv * HEAD_DIM, (kv + 1) * HEAD_DIM)
        k = _head_rms(k_ref[:, kcols], kn_ref[...])
        v = v_ref[:, kcols]
        ko_ref[:, kcols] = k
        vo_ref[:, kcols] = v
        kb = k.astype(BF16)
        vb = _with_ones(v.astype(BF16))
        for g in range(N_GROUP):
            head = kv * N_GROUP + g
            cols = slice(head * HEAD_DIM, (head + 1) * HEAD_DIM)
            q = _head_rms(q_ref[:, cols], qn_ref[...])
            o_ref[:, cols] = _softmax_pv(q, kb, vb).astype(o_ref.dtype)


def _attn_ctx(za, q_norm, k_norm, layer):
    return pl.pallas_call(
        _attn_ctx_kernel,
        grid=(BATCH,),
        in_specs=[
            pl.BlockSpec((SEQ, ATTN_DIM), lambda b: (b, W_Q // ATTN_DIM)),
            pl.BlockSpec((SEQ, KV_DIM), lambda b: (b, W_K // KV_DIM)),
            pl.BlockSpec((SEQ, KV_DIM), lambda b: (b, W_V // KV_DIM)),
            pl.BlockSpec((None, 1, HEAD_DIM), lambda b: (layer, 0, 0)),
            pl.BlockSpec((None, 1, HEAD_DIM), lambda b: (layer, 0, 0)),
        ],
        out_specs=[
            pl.BlockSpec((SEQ, ATTN_DIM), lambda b: (b, 0)),
            pl.BlockSpec((None, SEQ, KV_DIM), lambda b: (b, 0, 0)),
            pl.BlockSpec((None, SEQ, KV_DIM), lambda b: (b, 0, 0)),
        ],
        out_shape=[
            jax.ShapeDtypeStruct((N_CTX, ATTN_DIM), BF16),
            jax.ShapeDtypeStruct((BATCH, SEQ, KV_DIM), F32),
            jax.ShapeDtypeStruct((BATCH, SEQ, KV_DIM), F32),
        ],
        compiler_params=_params(("arbitrary",), 24 << 20),
        name="attn_ctx",
    )(za, za, za, q_norm.reshape(DEPTH, 1, HEAD_DIM), k_norm.reshape(DEPTH, 1, HEAD_DIM))


def _attn_lat_kernel(q_ref, k_ref, v_ref, ck_ref, cv_ref, qn_ref, kn_ref, cosq_ref, sinq_ref,
                     cosk_ref, sink_ref, o_ref, kb_ref, vb_ref):
    @pl.when(pl.program_id(2) == 0)
    def _():
        kb_ref[0:PAST_LEN, :] = ck_ref[...].astype(BF16)
        vb_ref[0:PAST_LEN, :] = _with_ones(cv_ref[...].astype(BF16))
        k = _rope(_head_rms(k_ref[...], kn_ref[...]), cosk_ref[...], sink_ref[...])
        kb_ref[PAST_LEN:, :] = k.astype(BF16)
        vb_ref[PAST_LEN:, :] = _with_ones(v_ref[...].astype(BF16))

    kb = kb_ref[...]
    vb = vb_ref[...]
    for g in range(N_GROUP):
        cols = slice(g * HEAD_DIM, (g + 1) * HEAD_DIM)
        q = _rope(_head_rms(q_ref[:, cols], qn_ref[...]), cosq_ref[...], sinq_ref[...])
        o_ref[:, cols] = _softmax_pv(q, kb, vb).astype(o_ref.dtype)


def _attn_lat(za, cache_k, cache_v, q_norm, k_norm, layer):
    gw = N_GROUP * HEAD_DIM
    tq = 256
    nq = DEC_SEQ // tq
    cos, sin = _rope_tables(DEC_SEQ)
    ck = cache_k.reshape(DEC_BATCH, DEPTH, PAST_LEN, KV_DIM)
    cv = cache_v.reshape(DEC_BATCH, DEPTH, PAST_LEN, KV_DIM)
    seq0 = N_CTX // DEC_SEQ
    tile0 = N_CTX // tq
    return pl.pallas_call(
        _attn_lat_kernel,
        grid=(DEC_BATCH, N_KV_HEADS, nq),
        in_specs=[
            pl.BlockSpec((tq, gw), lambda b, h, i: (tile0 + b * nq + i, W_Q // gw + h)),
            pl.BlockSpec((DEC_SEQ, HEAD_DIM), lambda b, h, i: (seq0 + b, W_K // HEAD_DIM + h)),
            pl.BlockSpec((DEC_SEQ, HEAD_DIM), lambda b, h, i: (seq0 + b, W_V // HEAD_DIM + h)),
            pl.BlockSpec((None, None, PAST_LEN, HEAD_DIM), lambda b, h, i: (b, layer, 0, h)),
            pl.BlockSpec((None, None, PAST_LEN, HEAD_DIM), lambda b, h, i: (b, layer, 0, h)),
            pl.BlockSpec((None, 1, HEAD_DIM), lambda b, h, i: (layer, 0, 0)),
            pl.BlockSpec((None, 1, HEAD_DIM), lambda b, h, i: (layer, 0, 0)),
            pl.BlockSpec((tq, HEAD_DIM), lambda b, h, i: (i, 0)),
            pl.BlockSpec((tq, HEAD_DIM), lambda b, h, i: (i, 0)),
            pl.BlockSpec((DEC_SEQ, HEAD_DIM), lambda b, h, i: (0, 0)),
            pl.BlockSpec((DEC_SEQ, HEAD_DIM), lambda b, h, i: (0, 0)),
        ],
        out_specs=pl.BlockSpec((tq, gw), lambda b, h, i: (b * nq + i, h)),
        out_shape=jax.ShapeDtypeStruct((N_LAT, ATTN_DIM), BF16),
        scratch_shapes=[pltpu.VMEM((PAST_LEN + DEC_SEQ, HEAD_DIM), BF16),
                        pltpu.VMEM((PAST_LEN + DEC_SEQ, 2 * HEAD_DIM), BF16)],
        compiler_params=_params(("arbitrary", "arbitrary", "arbitrary"), 32 << 20),
        name="attn_lat",
    )(za, za, za, ck, cv, q_norm.reshape(DEPTH, 1, HEAD_DIM), k_norm.reshape(DEPTH, 1, HEAD_DIM),
      cos, sin, cos, sin)


def _gla_consts():
    c = GLA_CHUNK
    t = np.arange(c)
    cum_f = (t[None, :] <= t[:, None]).astype(np.float32)
    cum_b = (t[None, :] >= t[:, None]).astype(np.float32)
    upper, same = [], []
    for level in range(GLA_LEVELS):
        n = c >> level
        blk = t // n
        p = blk * n + n // 2
        upper.append(np.broadcast_to((t >= p).astype(np.float32)[:, None], (c, GLA_DK)))
        same.append((blk[:, None] == blk[None, :]).astype(np.float32))
    same.append(2.0 * np.eye(c, dtype=np.float32))
    return (jnp.asarray(cum_f, BF16), jnp.asarray(cum_b, BF16),
            jnp.asarray(np.stack(upper + [1.0 - u for u in upper]), F32), jnp.asarray(np.stack(same), F32))


def _chunk_select(t_len):
    nc = t_len // GLA_CHUNK
    sel = (np.arange(t_len)[:, None] // GLA_CHUNK == np.arange(LANES)[None, :]).astype(np.float32)
    assert nc <= LANES
    return jnp.asarray(sel, BF16)


def _split_hi_lo(x):
    hi = x.astype(BF16)
    lo = (x - hi.astype(F32)).astype(BF16)
    return jnp.concatenate([hi, lo], axis=1)


def _pivot_rows(b, level):
    c = GLA_CHUNK
    n = c >> level
    if n >= 16:
        parts = [jnp.broadcast_to(b[s + n // 2:s + n // 2 + 1, :], (n, GLA_DK)) for s in range(0, c, n)]
        return parts[0] if len(parts) == 1 else jnp.concatenate(parts, axis=0)
    b3 = b.reshape(c // 8, 8, GLA_DK)
    if n == 8:
        return jnp.broadcast_to(b3[:, 4:5, :], b3.shape).reshape(c, GLA_DK)
    if n == 4:
        sub = lax.broadcasted_iota(jnp.int32, b3.shape, 1)
        lo = jnp.broadcast_to(b3[:, 2:3, :], b3.shape)
        hi = jnp.broadcast_to(b3[:, 6:7, :], b3.shape)
        return jnp.where(sub < 4, lo, hi).reshape(c, GLA_DK)
    row = lax.broadcasted_iota(jnp.int32, b.shape, 0)
    return jnp.where((row & 1) == 0, pltpu.roll(b, c - 1, 0), b)


def _gla_decay_stage(h, ci, q_ref, k_ref, bf_ref, bb_ref, up_ref, qc_ref, slot):
    qt_ref, kt_ref, kd_ref = slot
    c = GLA_CHUNK
    rows = slice(ci * c, (ci + 1) * c)
    hk = slice(h * GLA_DK, (h + 1) * GLA_DK)
    q = q_ref[rows, hk] * (GLA_DK ** -0.5)
    k = k_ref[rows, hk]
    b_f = bf_ref[rows, hk]
    b_b = bb_ref[rows, hk]
    for level in range(GLA_LEVELS):
        wf = jnp.exp(-jnp.abs(b_f - _pivot_rows(b_f, level)))
        wb = jnp.exp(-jnp.abs(b_b - _pivot_rows(b_b, level)))
        up = up_ref[level]
        dn = up_ref[GLA_LEVELS + level]
        lv = slice(c * level, c * (level + 1))
        qt_ref[lv, :] = jnp.concatenate([q * (wf * up), q * (wb * dn)], axis=1).astype(BF16)
        kt_ref[lv, :] = jnp.concatenate([k * (wf * dn), k * (wb * up)], axis=1).astype(BF16)
    qc_ref[rows, h * GLA_DV:(h + 1) * GLA_DV] = jnp.concatenate(
        [q * jnp.exp(b_f), q * jnp.exp(b_b)], axis=1).astype(BF16)
    kd_ref[...] = jnp.concatenate([k * jnp.exp(b_f[c - 1:c, :] - b_f),
                                   k * jnp.exp(b_b[0:1, :] - b_b)], axis=1).astype(BF16)


def _head_cols(refs, h):
    width = refs[0].shape[1]
    start = h * GLA_DV
    return refs[start // width], slice(start % width, start % width + GLA_DV)


def _gla_matmul_stage(h, ci, nc, q_ref, k_ref, v_refs, same_ref, oi_ref, kv_ref, slot):
    qt_ref, kt_ref, kd_ref = slot
    c = GLA_CHUNK
    rows = slice(ci * c, (ci + 1) * c)
    hk = slice(h * GLA_DK, (h + 1) * GLA_DK)
    q = (q_ref[rows, hk] * (GLA_DK ** -0.5)).astype(BF16)
    k = k_ref[rows, hk].astype(BF16)
    v_ref, vcols = _head_cols(v_refs, h)
    v = v_ref[rows, vcols].astype(BF16)
    att = _dot_nt(q, k) * same_ref[GLA_LEVELS]
    for level in range(GLA_LEVELS):
        lv = slice(c * level, c * (level + 1))
        att = att + _dot_nt(qt_ref[lv, :], kt_ref[lv, :]) * same_ref[level]
    oi_ref[rows, h * GLA_DV:(h + 1) * GLA_DV] = _dot(att.astype(BF16), v)
    kv_ref[h * nc + ci] = _dot_tn(kd_ref[...], v)


def _gla_kernel(*refs, t_len, hps, nv, has_state):
    refs = list(refs)
    q_ref, k_ref = refs[0:2]
    v_refs = refs[2:2 + nv]
    r_refs = refs[2 + nv:2 + 2 * nv]
    (zr_ref, wf_ref, wb_ref, bf_in_ref, bb_in_ref, gn_ref,
     cumf_ref, cumb_ref, up_ref, same_ref, sel_ref) = refs[2 + 2 * nv:13 + 2 * nv]
    rest = refs[13 + 2 * nv:]
    if has_state:
        s0f_ref, s0b_ref = rest[0:2]
        rest = rest[2:]
    (o_ref, sf_ref, sb_ref, bf_ref, bb_ref, oi_ref, qc_ref, kv_ref, st_ref,
     qt0, kt0, kd0, qt1, kt1, kd1) = rest
    c = GLA_CHUNK
    nc = t_len // c
    width = hps * GLA_DK
    slots = ((qt0, kt0, kd0), (qt1, kt1, kd1))
    zr = zr_ref[...].astype(BF16)
    laf = _log_sigmoid(_dot(zr, wf_ref[...]) + bf_in_ref[...]) * (1.0 / GLA_GATE_TAU)
    lab = _log_sigmoid(_dot(zr, wb_ref[...]) + bb_in_ref[...]) * (1.0 / GLA_GATE_TAU)
    bf_ref[...] = laf
    bb_ref[...] = lab

    def chunk_totals(la):
        s = _dot_tn(_split_hi_lo(la), sel_ref[...])
        return jnp.exp(s[:width, :] + s[width:, :])

    tot_f = chunk_totals(laf)
    tot_b = chunk_totals(lab)

    for ci in range(nc):
        rows = slice(ci * c, (ci + 1) * c)
        sf = _dot(cumf_ref[...], _split_hi_lo(bf_ref[rows, :]))
        sb = _dot(cumb_ref[...], _split_hi_lo(bb_ref[rows, :]))
        bf_ref[rows, :] = sf[:, :width] + sf[:, width:]
        bb_ref[rows, :] = sb[:, :width] + sb[:, width:]

    decay = functools.partial(_gla_decay_stage, q_ref=q_ref, k_ref=k_ref, bf_ref=bf_ref, bb_ref=bb_ref,
                              up_ref=up_ref, qc_ref=qc_ref)
    matmuls = functools.partial(_gla_matmul_stage, nc=nc, q_ref=q_ref, k_ref=k_ref, v_refs=v_refs,
                                same_ref=same_ref, oi_ref=oi_ref, kv_ref=kv_ref)
    items = [(h, ci) for h in range(hps) for ci in range(nc)]
    decay(*items[0], slot=slots[0])
    for n, item in enumerate(items):
        if n + 1 < len(items):
            decay(*items[n + 1], slot=slots[(n + 1) % 2])
        matmuls(*item, slot=slots[n % 2])

    for h in range(hps):
        hk = slice(h * GLA_DK, (h + 1) * GLA_DK)
        s = s0f_ref[...] if has_state else jnp.zeros((GLA_DK, GLA_DV), F32)
        for ci in range(nc):
            st_ref[h * nc + ci, 0:GLA_DK, :] = s.astype(BF16)
            s = tot_f[hk, ci:ci + 1] * s + kv_ref[h * nc + ci, 0:GLA_DK, :]
        sf_ref[h] = s
        s = s0b_ref[...] if has_state else jnp.zeros((GLA_DK, GLA_DV), F32)
        for ci in reversed(range(nc)):
            st_ref[h * nc + ci, GLA_DK:2 * GLA_DK, :] = s.astype(BF16)
            s = tot_b[hk, ci:ci + 1] * s + kv_ref[h * nc + ci, GLA_DK:2 * GLA_DK, :]
        sb_ref[h] = s

    for h, ci in items:
        rows = slice(ci * c, (ci + 1) * c)
        hv = slice(h * GLA_DV, (h + 1) * GLA_DV)
        o = oi_ref[rows, hv] + _dot(qc_ref[rows, hv], st_ref[h * nc + ci])
        o = o * lax.rsqrt(jnp.mean(o * o, axis=-1, keepdims=True) + EPS) * gn_ref[...]
        r_ref, rcols = _head_cols(r_refs, h)
        r = r_ref[rows, rcols]
        o_ref[rows, hv] = (o * (r * _sigmoid(r))).astype(o_ref.dtype)


def _gla(za, wgf, wgb, b_gate_f, b_gate_b, gla_norm, layer, row0, n_seq, t_len, hps, s0f=None, s0b=None):
    has_state = s0f is not None
    assert not has_state or hps == 1
    cum_f, cum_b, upper, same = _gla_consts()
    sel = _chunk_select(t_len)
    seq0 = row0 // t_len
    nc = t_len // GLA_CHUNK
    width = hps * GLA_DK
    vw = min(hps, 2) * GLA_DV
    nv = hps * GLA_DV // vw
    const = lambda shape: pl.BlockSpec(shape, lambda b, h: (0,) * len(shape))
    wide = lambda col0, j: pl.BlockSpec((t_len, vw), lambda b, h: (seq0 + b, col0 // vw + h * nv + j))
    in_specs = (
        [pl.BlockSpec((t_len, width), lambda b, h: (seq0 + b, W_GQ // width + h)),
         pl.BlockSpec((t_len, width), lambda b, h: (seq0 + b, W_GK // width + h))]
        + [wide(W_GV, j) for j in range(nv)] + [wide(W_GR, j) for j in range(nv)]
        + [pl.BlockSpec((t_len, LANES), lambda b, h: (seq0 + b, W_RANK // LANES)),
           pl.BlockSpec((None, LANES, width), lambda b, h: (layer, 0, h)),
           pl.BlockSpec((None, LANES, width), lambda b, h: (layer, 0, h)),
           pl.BlockSpec((None, 1, width), lambda b, h: (layer, 0, h)),
           pl.BlockSpec((None, 1, width), lambda b, h: (layer, 0, h)),
           pl.BlockSpec((None, 1, GLA_DV), lambda b, h: (layer, 0, 0)),
           const(cum_f.shape), const(cum_b.shape), const(upper.shape), const(same.shape), const(sel.shape)])
    args = [za] * (3 + 2 * nv) + [wgf, wgb, b_gate_f.reshape(DEPTH, 1, GLA_K_DIM),
                                  b_gate_b.reshape(DEPTH, 1, GLA_K_DIM), gla_norm.reshape(DEPTH, 1, GLA_DV),
                                  cum_f, cum_b, upper, same, sel]
    if has_state:
        in_specs += [
            pl.BlockSpec((None, None, None, GLA_DK, GLA_DV), lambda b, h: (b, layer, h, 0, 0)),
            pl.BlockSpec((None, None, None, GLA_DK, GLA_DV), lambda b, h: (b, layer, h, 0, 0)),
        ]
        args += [s0f, s0b]
    return pl.pallas_call(
        functools.partial(_gla_kernel, t_len=t_len, hps=hps, nv=nv, has_state=has_state),
        grid=(n_seq, GLA_HEADS // hps),
        in_specs=in_specs,
        out_specs=[
            pl.BlockSpec((t_len, hps * GLA_DV), lambda b, h: (b, h)),
            pl.BlockSpec((None, hps, GLA_DK, GLA_DV), lambda b, h: (b, h, 0, 0)),
            pl.BlockSpec((None, hps, GLA_DK, GLA_DV), lambda b, h: (b, h, 0, 0)),
        ],
        out_shape=[
            jax.ShapeDtypeStruct((n_seq * t_len, GLA_V_DIM), BF16),
            jax.ShapeDtypeStruct((n_seq, GLA_HEADS, GLA_DK, GLA_DV), F32),
            jax.ShapeDtypeStruct((n_seq, GLA_HEADS, GLA_DK, GLA_DV), F32),
        ],
        scratch_shapes=[
            pltpu.VMEM((t_len, width), F32), pltpu.VMEM((t_len, width), F32),
            pltpu.VMEM((t_len, hps * GLA_DV), F32), pltpu.VMEM((t_len, hps * GLA_DV), BF16),
            pltpu.VMEM((hps * nc, 2 * GLA_DK, GLA_DV), F32), pltpu.VMEM((hps * nc, 2 * GLA_DK, GLA_DV), BF16),
        ] + 2 * [pltpu.VMEM((GLA_CHUNK * GLA_LEVELS, 2 * GLA_DK), BF16),
                 pltpu.VMEM((GLA_CHUNK * GLA_LEVELS, 2 * GLA_DK), BF16),
                 pltpu.VMEM((GLA_CHUNK, 2 * GLA_DK), BF16)],
        compiler_params=_params(("arbitrary", "arbitrary"), 40 << 20),
        name="gla_%d" % t_len,
    )(*args)


def _merge_kernel(fc_ref, fl_ref, ac_ref, al_ref, gc_ref, gl_ref, wa_ref, wb_ref, wc_ref,
                  ga_ref, gb_ref, gg_ref, o_ref, wa_s, wb_s, wc_s, *, ctx_tiles):
    i = pl.program_id(1)

    @pl.when(i == 0)
    def _():
        wa_s[...] = wa_ref[...].astype(BF16)
        wb_s[...] = wb_ref[...].astype(BF16)
        wc_s[...] = wc_ref[...].astype(BF16)

    def compute(f_ref, a_ref, g_ref):
        acc = _sigmoid(ga_ref[...].astype(F32)) * _dot(f_ref[...], wa_s[...])
        acc = acc + _sigmoid(gb_ref[...].astype(F32)) * _dot(a_ref[...], wb_s[...])
        acc = acc + _sigmoid(gg_ref[...].astype(F32)) * _dot(g_ref[...], wc_s[...])
        o_ref[...] = acc.astype(o_ref.dtype)

    @pl.when(i < ctx_tiles)
    def _():
        compute(fc_ref, ac_ref, gc_ref)

    @pl.when(i >= ctx_tiles)
    def _():
        compute(fl_ref, al_ref, gl_ref)


def _merge(fa, at, gl, w_fourier, w_attn, w_gla, zg, layer):
    tm, tn = 256, 1024
    k = F_DIM
    nj = D_MODEL // tn
    ctx_tiles = N_CTX // tm
    ctx = pl.BlockSpec((tm, k), lambda j, i: (jnp.minimum(i, ctx_tiles - 1), 0))
    lat = pl.BlockSpec((tm, k), lambda j, i: (jnp.maximum(i - ctx_tiles, 0), 0))
    wsp = pl.BlockSpec((None, k, tn), lambda j, i: (layer, 0, j))
    gate = lambda br: pl.BlockSpec((tm, tn), lambda j, i: (i, br * nj + j))
    return pl.pallas_call(
        functools.partial(_merge_kernel, ctx_tiles=ctx_tiles),
        grid=(nj, N_TOK // tm),
        in_specs=[ctx, lat, ctx, lat, ctx, lat, wsp, wsp, wsp, gate(0), gate(1), gate(2)],
        out_specs=pl.BlockSpec((tm, tn), lambda j, i: (i, j)),
        out_shape=jax.ShapeDtypeStruct((N_TOK, D_MODEL), BF16),
        scratch_shapes=[pltpu.VMEM((k, tn), BF16)] * 3,
        compiler_params=_params(("arbitrary", "arbitrary"),
                                2 * (6 * tm * k * 2 + 3 * k * tn * 4 + 3 * tm * tn * 4 + tm * tn * 2)
                                + 3 * k * tn * 2 + 4 * tm * tn * 4 + (8 << 20)),
        name="merge",
    )(fa[0], fa[1], at[0], at[1], gl[0], gl[1], w_fourier, w_attn, w_gla, zg, zg, zg)


def _convffn_up_kernel(h_ref, wg_ref, wv_ref, cwg_ref, cwv_ref, cbg_ref, cbv_ref, o_ref, wg_s, wv_s, *, tm):
    @pl.when(pl.program_id(1) == 0)
    def _():
        wg_s[...] = wg_ref[...].astype(BF16)
        wv_s[...] = wv_ref[...].astype(BF16)

    row0 = pl.program_id(1) * tm
    seq_len = jnp.where(row0 < N_CTX, SEQ, DEC_SEQ)
    h = h_ref[...]
    pos = lax.broadcasted_iota(jnp.int32, (tm, 1), 0) & (seq_len - 1)
    has_prev = (pos != 0).astype(F32)
    has_next = (pos != seq_len - 1).astype(F32)

    def conv(u, cw_ref, cb_ref):
        prev = pltpu.roll(u, 1, 0) * has_prev
        nxt = pltpu.roll(u, tm - 1, 0) * has_next
        return prev * cw_ref[0:1, :] + u * cw_ref[1:2, :] + nxt * cw_ref[2:3, :] + cb_ref[...]

    g = conv(_dot(h, wg_s[...]), cwg_ref, cbg_ref)
    val = conv(_dot(h, wv_s[...]), cwv_ref, cbv_ref)
    o_ref[...] = (g * _sigmoid(g) * val).astype(o_ref.dtype)


def _convffn_up(h, w_up, conv_w, conv_b, layer):
    tm, tn = ROW_TILE, 512
    k = D_MODEL
    nj = D_FF // tn
    cb = conv_b.reshape(DEPTH, 1, 2 * D_FF)
    return pl.pallas_call(
        functools.partial(_convffn_up_kernel, tm=tm),
        grid=(nj, N_TOK // tm),
        in_specs=[
            pl.BlockSpec((tm, k), lambda j, i: (i, 0)),
            pl.BlockSpec((None, k, tn), lambda j, i: (layer, 0, j)),
            pl.BlockSpec((None, k, tn), lambda j, i: (layer, 0, nj + j)),
            pl.BlockSpec((None, 3, tn), lambda j, i: (layer, 0, j)),
            pl.BlockSpec((None, 3, tn), lambda j, i: (layer, 0, nj + j)),
            pl.BlockSpec((None, 1, tn), lambda j, i: (layer, 0, j)),
            pl.BlockSpec((None, 1, tn), lambda j, i: (layer, 0, nj + j)),
        ],
        out_specs=pl.BlockSpec((tm, tn), lambda j, i: (i, j)),
        out_shape=jax.ShapeDtypeStruct((N_TOK, D_FF), BF16),
        scratch_shapes=[pltpu.VMEM((k, tn), BF16)] * 2,
        compiler_params=_params(("arbitrary", "arbitrary"),
                                2 * (tm * k * 2 + 2 * k * tn * 4 + tm * tn * 2) + 2 * k * tn * 2
                                + 8 * tm * tn * 4 + (8 << 20)),
        name="convffn_up",
    )(h, w_up, w_up, conv_w, conv_w, cb, cb)


def _final_norm_kernel(x_ref, g_ref, oc_ref, ol_ref, *, ctx_tiles):
    i = pl.program_id(0)
    x = x_ref[...]
    y = x * lax.rsqrt(jnp.mean(x * x, axis=-1, keepdims=True) + EPS) * g_ref[...]

    @pl.when(i < ctx_tiles)
    def _():
        oc_ref[...] = y

    @pl.when(i >= ctx_tiles)
    def _():
        ol_ref[...] = y


def _final_norm(x, g):
    tm = 512
    ctx_tiles = N_CTX // tm
    return pl.pallas_call(
        functools.partial(_final_norm_kernel, ctx_tiles=ctx_tiles),
        grid=(N_TOK // tm,),
        in_specs=[pl.BlockSpec((tm, D_MODEL), lambda m: (m, 0)),
                  pl.BlockSpec((1, D_MODEL), lambda m: (0, 0))],
        out_specs=[pl.BlockSpec((tm, D_MODEL), lambda m: (jnp.minimum(m, ctx_tiles - 1), 0)),
                   pl.BlockSpec((tm, D_MODEL), lambda m: (jnp.maximum(m - ctx_tiles, 0), 0))],
        out_shape=[jax.ShapeDtypeStruct((N_CTX, D_MODEL), F32), jax.ShapeDtypeStruct((N_LAT, D_MODEL), F32)],
        compiler_params=_params(("arbitrary",), 8 * tm * D_MODEL * 4 + (8 << 20)),
        name="final_norm",
    )(x, g.reshape(1, D_MODEL))


def _pad_gate_w(w_gate, row0):
    out = jnp.zeros((DEPTH, LANES, GLA_K_DIM), BF16)
    return out.at[:, row0:row0 + GLA_GATE_RANK, :].set(w_gate.astype(BF16))


def kernel(x_prompt, x_sample, cache_k, cache_v, state_gla_fwd, state_gla_bwd, c, c_ctx, w_ada, b_ada, norm1, w_in, q_norm, k_norm, w_fourier, w_attn, w_gate_f, b_gate_f, w_gate_b, b_gate_b, gla_norm, w_gla, w_out, norm2, w_up, conv_w, conv_b, w_down, final_norm):
    x = jnp.concatenate([x_prompt.reshape(N_CTX, D_MODEL), x_sample.reshape(N_LAT, D_MODEL)], axis=0)
    cvec = jnp.concatenate([c_ctx[None, :], c, jnp.zeros((8 - 1 - DEC_BATCH, D_MODEL), F32)], axis=0)
    mod = _ada(cvec, w_ada, b_ada).reshape(DEPTH, 8, 1, N_MOD * D_MODEL)
    w_t = jnp.swapaxes(w_in, 1, 2)
    wgf = _pad_gate_w(w_gate_f, 0)
    wgb = _pad_gate_w(w_gate_b, GLA_GATE_RANK)

    new_k, new_v, new_sf, new_sb = [], [], [], []
    for l in range(DEPTH):
        h = _modnorm(x, norm1, mod, l, 0, 1)
        za = _in_proj(h, w_t, l)
        zg = _gates_proj(h, w_t, l)
        fa = (_fnet(za, 0, BATCH, SEQ), _fnet(za, N_CTX, DEC_BATCH, DEC_SEQ))
        at_ctx, k_ctx, v_ctx = _attn_ctx(za, q_norm, k_norm, l)
        at = (at_ctx, _attn_lat(za, cache_k, cache_v, q_norm, k_norm, l))
        gl_ctx, sf, sb = _gla(za, wgf, wgb, b_gate_f, b_gate_b, gla_norm, l, 0, BATCH, SEQ, GLA_HEADS)
        gl_lat, _, _ = _gla(za, wgf, wgb, b_gate_f, b_gate_b, gla_norm, l, N_CTX, DEC_BATCH, DEC_SEQ, 1,
                            state_gla_fwd, state_gla_bwd)
        merged = _merge(fa, at, (gl_ctx, gl_lat), w_fourier, w_attn, w_gla, zg, l)
        x = _resid_proj(merged, w_out, x, mod, l, 2, ROW_TILE, 1024)
        h = _modnorm(x, norm2, mod, l, 3, 4)
        hmid = _convffn_up(h, w_up, conv_w, conv_b, l)
        x = _resid_proj(hmid, w_down, x, mod, l, 5, 256, 1024, weight_buffers=1)
        new_k.append(k_ctx)
        new_v.append(v_ctx)
        new_sf.append(sf)
        new_sb.append(sb)

    y_ctx, y_lat = _final_norm(x, final_norm)
    y_prompt = y_ctx.reshape(BATCH, SEQ, D_MODEL)
    y_sample = y_lat.reshape(DEC_BATCH, DEC_SEQ, D_MODEL)
    kv_shape = (BATCH, DEPTH, SEQ, N_KV_HEADS, HEAD_DIM)
    return (y_prompt, y_sample,
            jnp.stack(new_k, axis=1).reshape(kv_shape), jnp.stack(new_v, axis=1).reshape(kv_shape),
            jnp.stack(new_sf, axis=1), jnp.stack(new_sb, axis=1))
```
